```python
import math
import jax, jax.numpy as jnp
from jax import lax
import numpy as np

D_MODEL = 1024
BATCH = 8
SEQ = 2048
DEPTH = 2

RMS_EPS = 1e-5

SSD_EXPAND = 2
SSD_INNER = SSD_EXPAND * D_MODEL
SSD_HEAD_DIM = 64
SSD_HEADS = SSD_INNER // SSD_HEAD_DIM
SSD_STATE = 128
SSD_GROUPS = 4
SSD_HPG = SSD_HEADS // SSD_GROUPS
SSD_CONV = 4
SSD_CHUNK = 128
SSD_CONV_CH = SSD_INNER + 2 * SSD_GROUPS * SSD_STATE

ATTN_HEAD_DIM = 128
ATTN_KV_HEADS = D_MODEL // ATTN_HEAD_DIM
ATTN_PATTERNS = ((128, 1), (512, 4), (2048, 16))
ATTN_N_PAT = len(ATTN_PATTERNS)
ATTN_Q_HEADS = ATTN_N_PAT * ATTN_KV_HEADS
ATTN_OUT = ATTN_KV_HEADS * ATTN_HEAD_DIM
ATTN_BLOCK = 128
ROPE_THETA = 500000.0
ROPE_DIM = ATTN_HEAD_DIM // 4

FFN_HIDDEN = -(-8 * D_MODEL // (3 * 256)) * 256

_SEGMENTS = (SSD_INNER, SSD_CONV_CH, SSD_HEADS, ATTN_Q_HEADS * ATTN_HEAD_DIM,
             ATTN_OUT, ATTN_OUT, D_MODEL, D_MODEL)
IN_SPLITS = tuple(int(s) for s in np.cumsum(_SEGMENTS)[:-1])
N_IN = int(sum(_SEGMENTS))

kernel_name = "hybrid_ssd_dilated_attn_gated_block"


def rms_norm(x, w):
    xf = x.astype(jnp.float32)
    y = xf * lax.rsqrt(jnp.mean(xf * xf, axis=-1, keepdims=True) + RMS_EPS)
    return (y * w.astype(jnp.float32)).astype(x.dtype)


def rope_tables(seq):
    inv = ROPE_THETA ** (-jnp.arange(0, ROPE_DIM, 2, dtype=jnp.float32) / ROPE_DIM)
    ang = jnp.arange(seq, dtype=jnp.float32)[:, None] * inv[None, :]
    return jnp.cos(ang), jnp.sin(ang)


def apply_partial_rope(t, cos, sin):
    half = ROPE_DIM // 2
    shape = (cos.shape[0],) + (1,) * (t.ndim - 3) + (half,)
    c, s = cos.reshape(shape), sin.reshape(shape)
    t1, t2 = t[..., :half], t[..., half:ROPE_DIM]
    return jnp.concatenate([t1 * c - t2 * s, t2 * c + t1 * s, t[..., ROPE_DIM:]], axis=-1).astype(t.dtype)


def causal_depthwise_conv(u, w, b):
    out = lax.conv_general_dilated(
        u, w[:, None, :].astype(u.dtype), window_strides=(1,), padding=((SSD_CONV - 1, 0),),
        dimension_numbers=("NWC", "WIO", "NWC"), feature_group_count=u.shape[-1])
    return out + b.astype(u.dtype)


def ssd_chunked(xh, dt, a, bm, cm):
    bsz, s, _, p = xh.shape
    nc = s // SSD_CHUNK
    x = (xh * dt[..., None]).reshape(bsz, nc, SSD_CHUNK, SSD_GROUPS, SSD_HPG, p)
    a_dt = (dt * a).reshape(bsz, nc, SSD_CHUNK, SSD_GROUPS, SSD_HPG).transpose(0, 3, 4, 1, 2)
    bc = bm.reshape(bsz, nc, SSD_CHUNK, SSD_GROUPS, SSD_STATE)
    cc = cm.reshape(bsz, nc, SSD_CHUNK, SSD_GROUPS, SSD_STATE)
    a_cum = jnp.cumsum(a_dt, axis=-1)
    causal = jnp.tril(jnp.ones((SSD_CHUNK, SSD_CHUNK), dtype=bool))
    decay = jnp.exp(jnp.where(causal, a_cum[..., :, None] - a_cum[..., None, :], -jnp.inf))
    cb = jnp.einsum("bclgn,bcsgn->bgcls", cc, bc)
    y_diag = jnp.einsum("bgecls,bcsgep->bclgep", cb[:, :, None] * decay, x)
    decay_states = jnp.exp(a_cum[..., -1:] - a_cum).transpose(0, 3, 4, 1, 2)
    states = jnp.einsum("bclgn,bclgep->bcgepn", bc, x * decay_states[..., None])
    chunk_decay = jnp.exp(a_cum[..., -1]).transpose(3, 0, 1, 2)

    def step(h, inp):
        s_c, d_c = inp
        return h * d_c[..., None, None] + s_c, h

    h0 = jnp.zeros((bsz, SSD_GROUPS, SSD_HPG, p, SSD_STATE), x.dtype)
    _, states_in = lax.scan(step, h0, (states.transpose(1, 0, 2, 3, 4, 5), chunk_decay))
    state_decay = jnp.exp(a_cum).transpose(0, 3, 4, 1, 2)[..., None]
    y_off = jnp.einsum("bclgn,cbgepn->bclgep", cc, states_in) * state_decay
    return (y_diag + y_off).reshape(bsz, s, SSD_HEADS, p)


def ssd_mixer(z, xbc, dt_raw, conv_w, conv_b, dt_bias, a_log, d_skip, norm_w):
    bsz, s, _ = z.shape
    f32 = jnp.float32
    xbc = jax.nn.silu(causal_depthwise_conv(xbc, conv_w, conv_b)).astype(f32)
    xs, bm, cm = jnp.split(xbc, [SSD_INNER, SSD_INNER + SSD_GROUPS * SSD_STATE], axis=-1)
    xs = xs.reshape(bsz, s, SSD_HEADS, SSD_HEAD_DIM)
    bm = bm.reshape(bsz, s, SSD_GROUPS, SSD_STATE)
    cm = cm.reshape(bsz, s, SSD_GROUPS, SSD_STATE)
    dt = jax.nn.softplus(dt_raw.astype(f32) + dt_bias.astype(f32))
    a = -jnp.exp(a_log.astype(f32))
    y = ssd_chunked(xs, dt, a, bm, cm) + d_skip.astype(f32)[:, None] * xs
    y = y.reshape(bsz, s, SSD_INNER) * jax.nn.silu(z.astype(f32))
    yg = y.reshape(bsz, s, SSD_GROUPS, SSD_INNER // SSD_GROUPS)
    yg = yg * lax.rsqrt(jnp.mean(yg * yg, axis=-1, keepdims=True) + RMS_EPS)
    return (yg.reshape(bsz, s, SSD_INNER) * norm_w.astype(f32)).astype(z.dtype)


def dilated_window_attention(q, k, v, dilation, steps):
    bsz, s, h, dh = q.shape
    length = s // dilation
    nb = -(-length // ATTN_BLOCK)
    lp = nb * ATTN_BLOCK

    def strided(t):
        t = t.reshape(bsz, length, dilation, h, dh).transpose(0, 2, 3, 1, 4)
        return jnp.pad(t, ((0, 0), (0, 0), (0, 0), (0, lp - length), (0, 0)))

    def banded(t):
        tp = jnp.pad(t, ((0, 0), (0, 0), (0, 0), (ATTN_BLOCK, 0), (0, 0)))
        prev = tp[..., :lp, :].reshape(bsz, dilation, h, nb, ATTN_BLOCK, dh)
        cur = t.reshape(bsz, dilation, h, nb, ATTN_BLOCK, dh)
        return jnp.concatenate([prev, cur], axis=-2)

    qb = strided(q).reshape(bsz, dilation, h, nb, ATTN_BLOCK, dh)
    kb, vb = banded(strided(k)), banded(strided(v))
    scores = jnp.einsum("brhnqe,brhnke->brhnqk", qb, kb, preferred_element_type=jnp.float32) * (dh ** -0.5)
    blk = jnp.arange(nb)[:, None, None]
    qi = jnp.arange(ATTN_BLOCK)[None, :, None] + ATTN_BLOCK
    kj = jnp.arange(2 * ATTN_BLOCK)[None, None, :]
    dist = qi - kj
    mask = (dist >= 0) & (dist <= steps) & (blk * ATTN_BLOCK + kj >= ATTN_BLOCK)
    scores = jnp.where(mask, scores, -jnp.inf)
    lse = jax.nn.logsumexp(scores, axis=-1)
    probs = jnp.exp(scores - lse[..., None])
    out = jnp.einsum("brhnqk,brhnke->brhnqe", probs.astype(v.dtype), vb, preferred_element_type=jnp.float32)

    def unstrided(t):
        tail = t.shape[5:]
        t = t.reshape((bsz, dilation, h, lp) + tail)[:, :, :, :length]
        return jnp.moveaxis(t, 3, 1).reshape((bsz, s, h) + tail)

    return unstrided(out), unstrided(lse)


def dilated_attention_mixer(q, k, v, cos, sin):
    bsz, s, _ = q.shape
    q = apply_partial_rope(q.reshape(bsz, s, ATTN_N_PAT, ATTN_KV_HEADS, ATTN_HEAD_DIM), cos, sin)
    k = apply_partial_rope(k.reshape(bsz, s, ATTN_KV_HEADS, ATTN_HEAD_DIM), cos, sin)
    v = v.reshape(bsz, s, ATTN_KV_HEADS, ATTN_HEAD_DIM)
    outs, lses = [], []
    for g, (window, dilation) in enumerate(ATTN_PATTERNS):
        o, l = dilated_window_attention(q[:, :, g], k, v, dilation, window // dilation)
        outs.append(o)
        lses.append(l)
    weights = jax.nn.softmax(jnp.stack(lses), axis=0)
    o = jnp.sum(weights[..., None] * jnp.stack(outs), axis=0)
    return o.reshape(bsz, s, ATTN_OUT).astype(v.dtype)


def setup_inputs(seed: int = 0) -> dict:
    key = jax.random.key(seed)
    ks = jax.random.split(key, 16)
    f32 = jnp.float32

    def dense(k, shape, fan_in):
        return jax.random.normal(k, shape, f32) * fan_in ** -0.5

    def gain(k, shape):
        return 1.0 + 0.02 * jax.random.normal(k, shape, f32)

    dt0 = jnp.exp(jax.random.uniform(ks[5], (DEPTH, SSD_HEADS), f32, math.log(1e-3), math.log(1e-1)))
    return {
        "x": jax.random.normal(ks[0], (BATCH, SEQ, D_MODEL), f32),
        "norm_mix": gain(ks[1], (DEPTH, D_MODEL)),
        "w_in": dense(ks[2], (DEPTH, D_MODEL, N_IN), D_MODEL),
        "conv_w": dense(ks[3], (DEPTH, SSD_CONV, SSD_CONV_CH), SSD_CONV),
        "conv_b": 0.02 * jax.random.normal(ks[4], (DEPTH, SSD_CONV_CH), f32),
        "dt_bias": dt0 + jnp.log(-jnp.expm1(-dt0)),
        "a_log": jnp.log(jax.random.uniform(ks[6], (DEPTH, SSD_HEADS), f32, 1.0, 16.0)),
        "d_skip": gain(ks[7], (DEPTH, SSD_HEADS)),
        "ssd_norm": gain(ks[8], (DEPTH, SSD_INNER)),
        "w_ssd_branch": dense(ks[9], (DEPTH, SSD_INNER, D_MODEL), SSD_INNER),
        "w_attn_branch": dense(ks[10], (DEPTH, ATTN_OUT, D_MODEL), ATTN_OUT),
        "w_out": dense(ks[11], (DEPTH, D_MODEL, D_MODEL), D_MODEL),
        "norm_ffn": gain(ks[12], (DEPTH, D_MODEL)),
        "w_gate_up": dense(ks[13], (DEPTH, D_MODEL, 2 * FFN_HIDDEN), D_MODEL),
        "w_down": dense(ks[14], (DEPTH, FFN_HIDDEN, D_MODEL), FFN_HIDDEN),
        "norm_final": gain(ks[15], (D_MODEL,)),
    }


def reference(x, norm_mix, w_in, conv_w, conv_b, dt_bias, a_log, d_skip, ssd_norm,
              w_ssd_branch, w_attn_branch, w_out, norm_ffn, w_gate_up, w_down, norm_final):
    cos, sin = rope_tables(x.shape[1])
    h = x
    for layer in range(DEPTH):
        u = rms_norm(h, norm_mix[layer])
        proj = u @ w_in[layer]
        z, xbc, dt_raw, q, k, v, g_ssd, g_attn = jnp.split(proj, IN_SPLITS, axis=-1)
        y_ssd = ssd_mixer(z, xbc, dt_raw, conv_w[layer], conv_b[layer], dt_bias[layer],
                          a_log[layer], d_skip[layer], ssd_norm[layer])
        y_attn = dilated_attention_mixer(q, k, v, cos, sin)
        merged = (jax.nn.sigmoid(g_ssd) * (y_ssd @ w_ssd_branch[layer])
                  + jax.nn.sigmoid(g_attn) * (y_attn @ w_attn_branch[layer]))
        h = h + merged @ w_out[layer]
        u = rms_norm(h, norm_ffn[layer])
        gate, up = jnp.split(u @ w_gate_up[layer], 2, axis=-1)
        h = h + (jax.nn.silu(gate) * up) @ w_down[layer]
    return rms_norm(h, norm_final)
```

```python
import functools
import math

import jax
import jax.numpy as jnp
from jax import lax
from jax.experimental import pallas as pl
from jax.experimental.pallas import tpu as pltpu

F32 = jnp.float32
BF16 = jnp.bfloat16

D_MODEL = 1024
RMS_EPS = 1e-5

SSD_INNER = 2048
SSD_HEAD_DIM = 64
SSD_HEADS = 32
SSD_STATE = 128
SSD_GROUPS = 4
SSD_CONV = 4
SSD_CHUNK = 128
SSD_BC = 2 * SSD_GROUPS * SSD_STATE
SSD_CONV_CH = SSD_INNER + SSD_BC

ATTN_HEAD_DIM = 128
ATTN_HEADS = 8
ATTN_PATTERNS = ((128, 1), (512, 4), (2048, 16))
ATTN_BLOCK = 128
ROPE_THETA = 500000.0
ROPE_DIM = 32
ATTN_OUT = ATTN_HEADS * ATTN_HEAD_DIM

FFN_HIDDEN = 2816

PACK_W = 12 * 1024
PACK_Q, PACK_Z, PACK_K, PACK_V, PACK_GS, PACK_GA = 3, 6, 8, 9, 10, 11

LANES = 128
NEG_BIG = -1e30
VMEM_LIMIT = 56 * 1024 * 1024


def _cparams(*sem):
    return pltpu.CompilerParams(dimension_semantics=sem, vmem_limit_bytes=VMEM_LIMIT)


def _sigmoid(x):
    return 1.0 / (1.0 + jnp.exp(-x))


def _softplus(x):
    return jnp.maximum(x, 0.0) + jnp.log(1.0 + jnp.exp(-jnp.abs(x)))


def _rms_scale(x):
    return x * lax.rsqrt(jnp.mean(x * x, axis=-1, keepdims=True) + RMS_EPS)


def _rmsnorm_kernel(x_ref, w_ref, o_ref):
    o_ref[...] = (_rms_scale(x_ref[...]) * w_ref[...]).astype(o_ref.dtype)


def _rmsnorm(x, w, tm=1024):
    t, d = x.shape
    return pl.pallas_call(
        _rmsnorm_kernel,
        grid=(t // tm,),
        in_specs=[pl.BlockSpec((tm, d), lambda i: (i, 0)), pl.BlockSpec((1, d), lambda i: (0, 0))],
        out_specs=pl.BlockSpec((tm, d), lambda i: (i, 0)),
        out_shape=jax.ShapeDtypeStruct((t, d), BF16),
        compiler_params=_cparams("parallel"),
        name="rmsnorm",
    )(x, w.reshape(1, d))


def _inproj_kernel(u_ref, w_ref, wdt_ref, o_ref, dt_ref):
    u = u_ref[...]
    o_ref[...] = jnp.dot(u, w_ref[...], preferred_element_type=F32).astype(o_ref.dtype)

    @pl.when(pl.program_id(1) == 0)
    def _():
        dt_ref[...] = jnp.dot(u, wdt_ref[...], preferred_element_type=F32)


def _inproj(u, w_pack, w_dt, tm=1024, tn=1024):
    t, d = u.shape
    n = w_pack.shape[1]
    return pl.pallas_call(
        _inproj_kernel,
        grid=(t // tm, n // tn),
        in_specs=[
            pl.BlockSpec((tm, d), lambda i, j: (i, 0)),
            pl.BlockSpec((d, tn), lambda i, j: (0, j)),
            pl.BlockSpec((d, LANES), lambda i, j: (0, 0)),
        ],
        out_specs=[
            pl.BlockSpec((tm, tn), lambda i, j: (i, j)),
            pl.BlockSpec((tm, LANES), lambda i, j: (i, 0)),
        ],
        out_shape=[jax.ShapeDtypeStruct((t, n), BF16), jax.ShapeDtypeStruct((t, LANES), F32)],
        compiler_params=_cparams("parallel", "arbitrary"),
        name="inproj",
    )(u, w_pack, w_dt)


def _ssd_kernel(xbc_ref, z_ref, dt_ref, convw_ref, convb_ref, dtb_ref, alog_ref, dskip_ref, normw_ref,
                o_ref, xbuf_ref, state_ref, xs_ref, bc_ref, y_ref):
    q = SSD_CHUNK

    @pl.when(pl.program_id(1) == 0)
    def _():
        xbuf_ref[0:8, :] = jnp.zeros((8, SSD_CONV_CH), F32)
        state_ref[...] = jnp.zeros_like(state_ref)

    xbuf_ref[8:8 + q, :] = xbc_ref[...].astype(F32)
    cw = 512
    for c0 in range(0, SSD_CONV_CH, cw):
        acc = convb_ref[:, c0:c0 + cw]
        for k in range(SSD_CONV):
            acc = acc + convw_ref[k:k + 1, c0:c0 + cw] * xbuf_ref[5 + k:5 + k + q, c0:c0 + cw]
        act = acc * _sigmoid(acc)
        if c0 < SSD_INNER:
            xs_ref[:, c0:c0 + cw] = act
        else:
            bc_ref[:, c0 - SSD_INNER:c0 - SSD_INNER + cw] = act
    xbuf_ref[0:8, :] = xbuf_ref[q:q + 8, :]

    dt = _softplus(dt_ref[...] + dtb_ref[...])
    adt = dt * (-jnp.exp(alog_ref[...]))
    rowi = lax.broadcasted_iota(jnp.int32, (q, q), 0)
    coli = lax.broadcasted_iota(jnp.int32, (q, q), 1)
    acum = adt
    for s in (1, 2, 4, 8, 16, 32, 64):
        acum = acum + jnp.where(rowi >= s, pltpu.roll(acum, s, 0), 0.0)
    acum_t = acum.T
    dt_t = dt.T
    causal = rowi >= coli
    left = coli < SSD_HEAD_DIM

    for g in range(SSD_GROUPS):
        b_g = bc_ref[:, g * SSD_STATE:(g + 1) * SSD_STATE]
        c_g = bc_ref[:, (SSD_GROUPS + g) * SSD_STATE:(SSD_GROUPS + g + 1) * SSD_STATE]
        c_gb = c_g.astype(BF16)
        cb = lax.dot_general(c_gb, b_g.astype(BF16), (((1,), (1,)), ((), ())), preferred_element_type=F32)
        b_t = b_g.T
        for pp in range(SSD_HEADS // SSD_GROUPS // 2):
            pidx = g * 4 + pp
            psl = slice(pidx * LANES, (pidx + 1) * LANES)
            m_parts, s_parts, ecols, ealasts = [], [], [], []
            for h in (2 * pidx, 2 * pidx + 1):
                col = acum[:, h:h + 1]
                row = acum_t[h:h + 1, :]
                dtr = dt_t[h:h + 1, :]
                alast = acum[q - 1:q, h:h + 1]
                decay = jnp.exp(jnp.where(causal, col - row, NEG_BIG))
                m_parts.append(cb * decay * dtr)
                s_parts.append(b_t * (jnp.exp(alast - row) * dtr))
                ecols.append(jnp.exp(col))
                ealasts.append(jnp.exp(alast))
            lhs = jnp.concatenate([jnp.concatenate(m_parts, axis=1), jnp.concatenate(s_parts, axis=1)],
                                  axis=0).astype(BF16)
            xp = xs_ref[:, psl]
            rhs = jnp.concatenate([jnp.where(left, xp, 0.0), jnp.where(left, 0.0, xp)], axis=0).astype(BF16)
            res = jnp.dot(lhs, rhs, preferred_element_type=F32)
            st_old = state_ref[:, psl]
            y_off = jnp.dot(c_gb, st_old.astype(BF16), preferred_element_type=F32)
            y_ref[:, psl] = res[0:q] + y_off * jnp.where(left, ecols[0], ecols[1]) + dskip_ref[:, psl] * xp
            state_ref[:, psl] = st_old * jnp.where(left, ealasts[0], ealasts[1]) + res[q:2 * q]

    gw = SSD_INNER // SSD_GROUPS
    for g in range(SSD_GROUPS):
        gsl = slice(g * gw, (g + 1) * gw)
        zg = z_ref[:, gsl].astype(F32)
        yg = y_ref[:, gsl] * (zg * _sigmoid(zg))
        o_ref[:, gsl] = (_rms_scale(yg) * normw_ref[:, gsl]).astype(o_ref.dtype)


def _ssd(pack, dt_raw, conv_w, conv_b, dt_bias, a_log, d_skip, norm_w, batch, seq):
    t = pack.shape[0]
    nc = seq // SSD_CHUNK
    q = SSD_CHUNK
    pad = LANES - SSD_HEADS
    row = lambda b, c: b * nc + c
    const = lambda shape: pl.BlockSpec(shape, lambda b, c: (0, 0))
    return pl.pallas_call(
        _ssd_kernel,
        grid=(batch, nc),
        in_specs=[
            pl.BlockSpec((q, SSD_CONV_CH), lambda b, c: (row(b, c), 0)),
            pl.BlockSpec((q, SSD_INNER), lambda b, c: (row(b, c), PACK_Z // 2)),
            pl.BlockSpec((q, LANES), lambda b, c: (row(b, c), 0)),
            const((SSD_CONV, SSD_CONV_CH)), const((1, SSD_CONV_CH)), const((1, LANES)), const((1, LANES)),
            const((1, SSD_INNER)), const((1, SSD_INNER)),
        ],
        out_specs=pl.BlockSpec((q, SSD_INNER), lambda b, c: (row(b, c), 0)),
        out_shape=jax.ShapeDtypeStruct((t, SSD_INNER), BF16),
        scratch_shapes=[
            pltpu.VMEM((q + 8, SSD_CONV_CH), F32),
            pltpu.VMEM((SSD_STATE, SSD_INNER), F32),
            pltpu.VMEM((q, SSD_INNER), F32),
            pltpu.VMEM((q, SSD_BC), F32),
            pltpu.VMEM((q, SSD_INNER), F32),
        ],
        compiler_params=_cparams("parallel", "arbitrary"),
        name="ssd",
    )(pack, pack, dt_raw, conv_w, conv_b.reshape(1, -1),
      jnp.pad(dt_bias, (0, pad)).reshape(1, LANES), jnp.pad(a_log, (0, pad)).reshape(1, LANES),
      jnp.repeat(d_skip, SSD_HEAD_DIM).reshape(1, SSD_INNER), norm_w.reshape(1, SSD_INNER))


ATTN_HB = 4


def _attn_kernel(q_ref, k_ref, v_ref, cos_ref, sa_ref, sb_ref, o_ref, lse_ref,
                 nat_ref, tmp_ref, qrm_ref, krm_ref, vrm_ref, orm_ref, lrm_ref, *, dilation):
    assert dilation in (1, 4, 16)
    blk = ATTN_BLOCK
    seq = nat_ref.shape[0]
    length = seq // dilation
    nblk = length // blk
    quarter = seq // 4
    scale = ATTN_HEAD_DIM ** -0.5

    def rope(x):
        return (x * cos_ref[...] + pltpu.roll(x, ROPE_DIM // 2, 1) * sa_ref[...]
                + pltpu.roll(x, LANES - ROPE_DIM // 2, 1) * sb_ref[...])

    def split_rows(dst_ref):
        if dilation == 1:
            dst_ref[...] = nat_ref[...].astype(BF16)
        elif dilation == 4:
            for r in range(4):
                dst_ref[r * quarter:(r + 1) * quarter, :] = nat_ref[pl.ds(r, quarter, stride=4), :].astype(BF16)
        else:
            for r in range(4):
                tmp_ref[r * quarter:(r + 1) * quarter, :] = nat_ref[pl.ds(r, quarter, stride=4), :]
            for r in range(4):
                for r2 in range(4):
                    c = r * 4 + r2
                    dst_ref[c * length:(c + 1) * length, :] = (
                        tmp_ref[pl.ds(r * quarter + r2, length, stride=4), :].astype(BF16))

    def merge_rows(src_ref):
        if dilation == 1:
            nat_ref[...] = src_ref[...]
        elif dilation == 4:
            for r in range(4):
                nat_ref[pl.ds(r, quarter, stride=4), :] = src_ref[r * quarter:(r + 1) * quarter, :]
        else:
            for r in range(4):
                for r2 in range(4):
                    c = r * 4 + r2
                    tmp_ref[pl.ds(r * quarter + r2, length, stride=4), :] = src_ref[c * length:(c + 1) * length, :]
            for r in range(4):
                nat_ref[pl.ds(r, quarter, stride=4), :] = tmp_ref[r * quarter:(r + 1) * quarter, :]

    qi = lax.broadcasted_iota(jnp.int32, (blk, blk), 0)
    kj = lax.broadcasted_iota(jnp.int32, (blk, blk), 1)
    mask_cur = kj <= qi
    qi2 = lax.broadcasted_iota(jnp.int32, (blk, 2 * blk), 0)
    kj2 = lax.broadcasted_iota(jnp.int32, (blk, 2 * blk), 1)
    mask_two = (kj2 >= qi2) & (kj2 <= qi2 + blk)
    lane = lax.broadcasted_iota(jnp.int32, (blk, LANES), 1)
    nt = (((1,), (1,)), ((), ()))

    lrm_ref[...] = jnp.zeros_like(lrm_ref)

    def finish(h, r0, s, vv):
        m = jnp.max(s, axis=-1, keepdims=True)
        p = jnp.exp(s - m)
        l = jnp.sum(p, axis=-1, keepdims=True)
        orm_ref[pl.ds(r0, blk), :] = jnp.dot(p.astype(BF16), vv, preferred_element_type=F32) * (1.0 / l)
        lrm_ref[pl.ds(r0, blk), :] = jnp.where(lane == h, m + jnp.log(l), lrm_ref[pl.ds(r0, blk), :])

    def head_body(h, carry):
        hoff = pl.multiple_of(h * LANES, LANES)
        nat_ref[...] = rope(q_ref[0, :, pl.ds(hoff, LANES)].astype(F32)) * scale
        split_rows(qrm_ref)
        nat_ref[...] = rope(k_ref[0, :, pl.ds(hoff, LANES)].astype(F32))
        split_rows(krm_ref)
        nat_ref[...] = v_ref[0, :, pl.ds(hoff, LANES)].astype(F32)
        split_rows(vrm_ref)

        def res_body(res, c):
            base = pl.multiple_of(res * length, blk)
            s0 = lax.dot_general(qrm_ref[pl.ds(base, blk), :], krm_ref[pl.ds(base, blk), :], nt,
                                 preferred_element_type=F32)
            finish(h, base, jnp.where(mask_cur, s0, NEG_BIG), vrm_ref[pl.ds(base, blk), :])

            def blk_body(n, c2):
                r0 = pl.multiple_of(base + n * blk, blk)
                p0 = pl.multiple_of(r0 - blk, blk)
                s = lax.dot_general(qrm_ref[pl.ds(r0, blk), :], krm_ref[pl.ds(p0, 2 * blk), :], nt,
                                    preferred_element_type=F32)
                finish(h, r0, jnp.where(mask_two, s, NEG_BIG), vrm_ref[pl.ds(p0, 2 * blk), :])
                return c2

            if nblk > 1:
                lax.fori_loop(1, nblk, blk_body, 0)
            return c

        lax.fori_loop(0, dilation, res_body, 0)
        merge_rows(orm_ref)
        o_ref[0, :, pl.ds(hoff, LANES)] = nat_ref[...].astype(o_ref.dtype)
        return carry

    lax.fori_loop(0, ATTN_HB, head_body, 0)
    merge_rows(lrm_ref)
    lse_ref[0] = nat_ref[...]


def _attention_group(pack3, tables, g, dilation):
    batch, seq, _ = pack3.shape
    halves = ATTN_HEADS // ATTN_HB
    wb = ATTN_HB * ATTN_HEAD_DIM
    col = lambda unit: (lambda b, hh: (b, 0, unit * halves + hh))
    tab_spec = pl.BlockSpec((seq, LANES), lambda b, hh: (0, 0))
    out, lse = pl.pallas_call(
        functools.partial(_attn_kernel, dilation=dilation),
        grid=(batch, halves),
        in_specs=[
            pl.BlockSpec((1, seq, wb), col(PACK_Q + g)),
            pl.BlockSpec((1, seq, wb), col(PACK_K)),
            pl.BlockSpec((1, seq, wb), col(PACK_V)),
            tab_spec, tab_spec, tab_spec,
        ],
        out_specs=[
            pl.BlockSpec((1, seq, wb), lambda b, hh: (b, 0, hh)),
            pl.BlockSpec((1, seq, LANES), lambda b, hh: (b, 0, hh)),
        ],
        out_shape=[
            jax.ShapeDtypeStruct((batch, seq, ATTN_OUT), BF16),
            jax.ShapeDtypeStruct((batch, seq, halves * LANES), F32),
        ],
        scratch_shapes=[
            pltpu.VMEM((seq, LANES), F32),
            pltpu.VMEM((seq, LANES), F32),
            pltpu.VMEM((seq, LANES), BF16),
            pltpu.VMEM((seq, LANES), BF16),
            pltpu.VMEM((seq, LANES), BF16),
            pltpu.VMEM((seq, LANES), F32),
            pltpu.VMEM((seq, LANES), F32),
        ],
        compiler_params=_cparams("parallel", "parallel"),
        name=f"attn_d{dilation}",
    )(pack3, pack3, pack3, *tables)
    return out.reshape(batch * seq, ATTN_OUT), lse.reshape(batch * seq, halves * LANES)


def _rope_tables(seq):
    half = ROPE_DIM // 2
    inv = ROPE_THETA ** (-jnp.arange(0, ROPE_DIM, 2, dtype=F32) / ROPE_DIM)
    ang = jnp.arange(seq, dtype=F32)[:, None] * inv[None, :]
    c, s = jnp.cos(ang), jnp.sin(ang)
    zeros = jnp.zeros((seq, LANES - ROPE_DIM), F32)
    zh = jnp.zeros((seq, half), F32)
    cos = jnp.concatenate([c, c, jnp.ones((seq, LANES - ROPE_DIM), F32)], axis=1)
    sin_a = jnp.concatenate([zh, s, zeros], axis=1)
    sin_b = jnp.concatenate([-s, zh, zeros], axis=1)
    return cos, sin_a, sin_b


def _merge_kernel(yssd_ref, o0_ref, o1_ref, o2_ref, l0_ref, l1_ref, l2_ref, gs_ref, ga_ref, h_ref,
                  wa_ref, wb_ref, wo_ref, nw_ref, hout_ref, u_ref, yattn_ref):
    l0, l1, l2 = l0_ref[...], l1_ref[...], l2_ref[...]
    m = jnp.maximum(jnp.maximum(l0, l1), l2)
    e0, e1, e2 = jnp.exp(l0 - m), jnp.exp(l1 - m), jnp.exp(l2 - m)
    inv = 1.0 / (e0 + e1 + e2)
    w0, w1, w2 = e0 * inv, e1 * inv, e2 * inv
    for h in range(ATTN_HEADS):
        sl = slice(h * LANES, (h + 1) * LANES)
        c = (h // ATTN_HB) * LANES + h % ATTN_HB
        yh = (w0[:, c:c + 1] * o0_ref[:, sl].astype(F32) + w1[:, c:c + 1] * o1_ref[:, sl].astype(F32)
              + w2[:, c:c + 1] * o2_ref[:, sl].astype(F32))
        yattn_ref[:, sl] = yh.astype(BF16)
    a = jnp.dot(yssd_ref[...], wa_ref[...], preferred_element_type=F32)
    b = jnp.dot(yattn_ref[...], wb_ref[...], preferred_element_type=F32)
    merged = _sigmoid(gs_ref[...].astype(F32)) * a + _sigmoid(ga_ref[...].astype(F32)) * b
    hn = h_ref[...] + jnp.dot(merged.astype(BF16), wo_ref[...], preferred_element_type=F32)
    hout_ref[...] = hn
    u_ref[...] = (_rms_scale(hn) * nw_ref[...]).astype(u_ref.dtype)


def _merge(y_ssd, outs, lses, pack, h, w_a, w_b, w_o, norm_w, tm=256):
    t = h.shape[0]
    d = D_MODEL
    rows = lambda w, col=0: pl.BlockSpec((tm, w), lambda i, col=col: (i, col))
    whole = lambda a: pl.BlockSpec(a.shape, lambda i: (0, 0))
    nw = norm_w.reshape(1, d)
    lw = lses[0].shape[1]
    return pl.pallas_call(
        _merge_kernel,
        grid=(t // tm,),
        in_specs=[rows(SSD_INNER), rows(d), rows(d), rows(d), rows(lw), rows(lw), rows(lw),
                  rows(d, PACK_GS), rows(d, PACK_GA), rows(d), whole(w_a), whole(w_b), whole(w_o), whole(nw)],
        out_specs=[rows(d), rows(d)],
        out_shape=[jax.ShapeDtypeStruct((t, d), F32), jax.ShapeDtypeStruct((t, d), BF16)],
        scratch_shapes=[pltpu.VMEM((tm, ATTN_OUT), BF16)],
        compiler_params=_cparams("parallel"),
        name="merge",
    )(y_ssd, *outs, *lses, pack, pack, h, w_a, w_b, w_o, nw)


def _ffn_kernel(u_ref, wg_ref, wu_ref, wd_ref, h_ref, nw_ref, *rest, last):
    if last:
        out_ref, acc_ref = rest
    else:
        hout_ref, u_out_ref, acc_ref = rest
    j = pl.program_id(1)

    @pl.when(j == 0)
    def _():
        acc_ref[...] = h_ref[...]

    u = u_ref[...]
    gate = jnp.dot(u, wg_ref[...], preferred_element_type=F32)
    up = jnp.dot(u, wu_ref[...], preferred_element_type=F32)
    act = (gate * _sigmoid(gate) * up).astype(BF16)
    acc_ref[...] += jnp.dot(act, wd_ref[...], preferred_element_type=F32)

    @pl.when(j == pl.num_programs(1) - 1)
    def _():
        hn = acc_ref[...]
        normed = _rms_scale(hn) * nw_ref[...]
        if last:
            out_ref[...] = normed
        else:
            hout_ref[...] = hn
            u_out_ref[...] = normed.astype(u_out_ref.dtype)


def _ffn(u, w_gu, w_d, h, norm_w, last, tm=1024, th=256):
    t, d = u.shape
    nh = FFN_HIDDEN // th
    rows = lambda: pl.BlockSpec((tm, d), lambda i, j: (i, 0))
    if last:
        out_specs = rows()
        out_shape = jax.ShapeDtypeStruct((t, d), F32)
    else:
        out_specs = [rows(), rows()]
        out_shape = [jax.ShapeDtypeStruct((t, d), F32), jax.ShapeDtypeStruct((t, d), BF16)]
    return pl.pallas_call(
        functools.partial(_ffn_kernel, last=last),
        grid=(t // tm, nh),
        in_specs=[
            rows(),
            pl.BlockSpec((d, th), lambda i, j: (0, j)),
            pl.BlockSpec((d, th), lambda i, j: (0, nh + j)),
            pl.BlockSpec((th, d), lambda i, j: (j, 0)),
            rows(),
            pl.BlockSpec((1, d), lambda i, j: (0, 0)),
        ],
        out_specs=out_specs,
        out_shape=out_shape,
        scratch_shapes=[pltpu.VMEM((tm, d), F32)],
        compiler_params=_cparams("parallel", "arbitrary"),
        name="ffn_last" if last else "ffn",
    )(u, w_gu, w_gu, w_d, h, norm_w.reshape(1, d))


def _pack_in_weights(w):
    z0, xbc0, dt0, q0, k0, v0, gs0, ga0, end = 0, 2048, 5120, 5152, 8224, 9248, 10272, 11296, 12320
    packed = jnp.concatenate([w[:, xbc0:dt0], w[:, q0:k0], w[:, z0:xbc0], w[:, k0:end]], axis=1)
    w_dt = jnp.pad(w[:, dt0:q0], ((0, 0), (0, LANES - SSD_HEADS)))
    return packed.astype(BF16), w_dt.astype(BF16)


def kernel(x, norm_mix, w_in, conv_w, conv_b, dt_bias, a_log, d_skip, ssd_norm, w_ssd_branch, w_attn_branch,
           w_out, norm_ffn, w_gate_up, w_down, norm_final):
    batch, seq, d = x.shape
    depth = w_in.shape[0]
    t = batch * seq
    tables = _rope_tables(seq)
    h = x.reshape(t, d)
    u = _rmsnorm(h, norm_mix[0])
    for layer in range(depth):
        w_pack, w_dt = _pack_in_weights(w_in[layer])
        pack, dt_raw = _inproj(u, w_pack, w_dt)
        y_ssd = _ssd(pack, dt_raw, conv_w[layer], conv_b[layer], dt_bias[layer], a_log[layer], d_skip[layer],
                     ssd_norm[layer], batch, seq)
        pack3 = pack.reshape(batch, seq, PACK_W)
        outs, lses = [], []
        for g, (window, dilation) in enumerate(ATTN_PATTERNS):
            assert window // dilation == ATTN_BLOCK
            o, l = _attention_group(pack3, tables, g, dilation)
            outs.append(o)
            lses.append(l)
        h, u_ffn = _merge(y_ssd, outs, lses, pack, h, w_ssd_branch[layer].astype(BF16),
                          w_attn_branch[layer].astype(BF16), w_out[layer].astype(BF16), norm_ffn[layer])
        last = layer == depth - 1
        nxt = norm_final if last else norm_mix[layer + 1]
        res = _ffn(u_ffn, w_gate_up[layer].astype(BF16), w_down[layer].astype(BF16), h, nxt, last)
        if last:
            return res.reshape(batch, seq, d)
        h, u = res
```

```python
import functools
import math

import jax
import jax.numpy as jnp
from jax import lax
from jax.experimental import pallas as pl
from jax.experimental.pallas import tpu as pltpu

F32 = jnp.float32
BF16 = jnp.bfloat16

D_MODEL = 1024
RMS_EPS = 1e-5

SSD_INNER = 2048
SSD_HEAD_DIM = 64
SSD_HEADS = 32
SSD_STATE = 128
SSD_GROUPS = 4
SSD_CONV = 4
SSD_CHUNK = 128
SSD_BC = 2 * SSD_GROUPS * SSD_STATE
SSD_CONV_CH = SSD_INNER + SSD_BC

ATTN_HEAD_DIM = 128
ATTN_HEADS = 8
ATTN_PATTERNS = ((128, 1), (512, 4), (2048, 16))
ATTN_BLOCK = 128
ROPE_THETA = 500000.0
ROPE_DIM = 32
ATTN_OUT = ATTN_HEADS * ATTN_HEAD_DIM

FFN_HIDDEN = 2816

PACK_W = 12 * 1024
PACK_Q, PACK_Z, PACK_K, PACK_V, PACK_GS, PACK_GA = 3, 6, 8, 9, 10, 11

LANES = 128
NEG_BIG = -1e30
VMEM_LIMIT = 56 * 1024 * 1024


def _cparams(*sem):
    return pltpu.CompilerParams(dimension_semantics=sem, vmem_limit_bytes=VMEM_LIMIT)


def _sigmoid(x):
    return 1.0 / (1.0 + jnp.exp(-x))


def _silu(x):
    h = 0.5 * x
    return h + h * jnp.tanh(h)


def _softplus(x):
    return jnp.maximum(x, 0.0) + jnp.log(1.0 + jnp.exp(-jnp.abs(x)))


def _rms_scale(x):
    return x * lax.rsqrt(jnp.mean(x * x, axis=-1, keepdims=True) + RMS_EPS)


def _rmsnorm_kernel(x_ref, w_ref, o_ref):
    o_ref[...] = (_rms_scale(x_ref[...]) * w_ref[...]).astype(o_ref.dtype)


def _rmsnorm(x, w, tm=1024):
    t, d = x.shape
    return pl.pallas_call(
        _rmsnorm_kernel,
        grid=(t // tm,),
        in_specs=[pl.BlockSpec((tm, d), lambda i: (i, 0)), pl.BlockSpec((1, d), lambda i: (0, 0))],
        out_specs=pl.BlockSpec((tm, d), lambda i: (i, 0)),
        out_shape=jax.ShapeDtypeStruct((t, d), BF16),
        compiler_params=_cparams("parallel"),
        name="rmsnorm",
    )(x, w.reshape(1, d))


def _inproj_kernel(u_ref, w_ref, wdt_ref, cos_ref, sin_ref, o_ref, dt_ref):
    j = pl.program_id(1)
    is_q = (j >= PACK_Q) & (j < PACK_Q + len(ATTN_PATTERNS))
    rotary = is_q | (j == PACK_K)

    @pl.when(rotary)
    def _():
        res = jnp.dot(u_ref[...], w_ref[...], preferred_element_type=F32)
        scale = jnp.where(is_q, ATTN_HEAD_DIM ** -0.5, 1.0)
        cos, sin = cos_ref[...] * scale, sin_ref[...] * scale
        for h in range(res.shape[1] // LANES):
            x = res[:, h * LANES:(h + 1) * LANES]
            o_ref[:, h * LANES:(h + 1) * LANES] = (x * cos + pltpu.roll(x, LANES // 2, 1) * sin).astype(o_ref.dtype)

    @pl.when(jnp.logical_not(rotary))
    def _():
        o_ref[...] = jnp.dot(u_ref[...], w_ref[...], preferred_element_type=F32).astype(o_ref.dtype)

    @pl.when(j == 0)
    def _():
        dt_ref[...] = jnp.dot(u_ref[...], wdt_ref[...], preferred_element_type=F32)


def _inproj(u, w_pack, w_dt, tables, tm=1024):
    t, d = u.shape
    n = w_pack.shape[1]
    tn = ATTN_OUT
    seq = tables[0].shape[0]
    tab_spec = pl.BlockSpec((tm, LANES), lambda i, j: (i % (seq // tm), 0))
    return pl.pallas_call(
        _inproj_kernel,
        grid=(t // tm, n // tn),
        in_specs=[
            pl.BlockSpec((tm, d), lambda i, j: (i, 0)),
            pl.BlockSpec((d, tn), lambda i, j: (0, j)),
            pl.BlockSpec((d, LANES), lambda i, j: (0, 0)),
            tab_spec, tab_spec,
        ],
        out_specs=[
            pl.BlockSpec((tm, tn), lambda i, j: (i, j)),
            pl.BlockSpec((tm, LANES), lambda i, j: (i, 0)),
        ],
        out_shape=[jax.ShapeDtypeStruct((t, n), BF16), jax.ShapeDtypeStruct((t, LANES), F32)],
        compiler_params=_cparams("parallel", "arbitrary"),
        name="inproj",
    )(u, w_pack, w_dt, *tables)


def _ssd_kernel(xbc_ref, z_ref, dt_ref, convw_ref, convb_ref, dtb_ref, alog_ref, dskip_ref, normw_ref,
                o_ref, xbuf_ref, state_ref, conv_ref, y_ref, acum_ref, rowm_ref):
    q = SSD_CHUNK
    hpg = SSD_HEADS // SSD_GROUPS

    @pl.when(pl.program_id(1) == 0)
    def _():
        xbuf_ref[0:q, :] = jnp.zeros((q, SSD_CONV_CH), BF16)
        state_ref[...] = jnp.zeros_like(state_ref)

    xbuf_ref[q:2 * q, :] = xbc_ref[...]
    ti = lax.broadcasted_iota(jnp.int32, (3 * q, 2 * q), 0)
    si = lax.broadcasted_iota(jnp.int32, (3 * q, 2 * q), 1)
    tap = lax.shift_right_logical(ti, q.bit_length() - 1)
    shift = jnp.where(si == (ti & (q - 1)) + (q - 3) + tap, 1.0, 0.0).astype(BF16)
    cw = 512

    def conv_body(i, c):
        cs = pl.ds(pl.multiple_of(i * cw, cw), cw)
        xx = xbuf_ref[:, cs]
        sh = jnp.dot(shift, xx, preferred_element_type=F32)
        w = convw_ref[:, cs]
        acc = (convb_ref[:, cs] + w[3:4] * xx[q:2 * q].astype(F32) + w[0:1] * sh[0:q] + w[1:2] * sh[q:2 * q]
               + w[2:3] * sh[2 * q:3 * q])
        conv_ref[:, cs] = _silu(acc)
        return c

    lax.fori_loop(0, SSD_CONV_CH // cw, conv_body, 0, unroll=3)
    xbuf_ref[0:q, :] = xbuf_ref[q:2 * q, :]

    dt = _softplus(dt_ref[...] + dtb_ref[...])
    adt = dt * (-jnp.exp(alog_ref[...]))
    rowi = lax.broadcasted_iota(jnp.int32, (q, q), 0)
    coli = lax.broadcasted_iota(jnp.int32, (q, q), 1)
    acum = adt
    for s in (1, 2, 4, 8, 16, 32, 64):
        acum = acum + jnp.where(rowi >= s, pltpu.roll(acum, s, 0), 0.0)
    rowm_ref[...] = acum.T - jnp.log(dt.T)
    for g in range(SSD_GROUPS):
        acum_ref[g] = pltpu.roll(acum, (LANES - hpg * g) % LANES, 1)
    causal = rowi >= coli
    left = coli < SSD_HEAD_DIM

    def group_body(g, c):
        a_g = acum_ref[g]
        rm_g = rowm_ref[pl.ds(pl.multiple_of(g * hpg, hpg), hpg), :]
        b_g = conv_ref[:, pl.ds(pl.multiple_of(SSD_INNER + g * SSD_STATE, LANES), SSD_STATE)]
        c_g = conv_ref[:, pl.ds(pl.multiple_of(SSD_INNER + (SSD_GROUPS + g) * SSD_STATE, LANES), SSD_STATE)]
        c_gb = c_g.astype(BF16)
        cb = lax.dot_general(c_gb, b_g.astype(BF16), (((1,), (1,)), ((), ())), preferred_element_type=F32)
        b_t = b_g.T
        for pp in range(hpg // 2):
            ps = pl.ds(pl.multiple_of((g * (hpg // 2) + pp) * LANES, LANES), LANES)
            m_parts, s_parts, ecols, ealasts = [], [], [], []
            for j in (2 * pp, 2 * pp + 1):
                col = a_g[:, j:j + 1]
                rowm = rm_g[j:j + 1, :]
                alast = a_g[q - 1:q, j:j + 1]
                m_parts.append(cb * jnp.exp(jnp.where(causal, col - rowm, NEG_BIG)))
                s_parts.append(b_t * jnp.exp(alast - rowm))
                ecols.append(jnp.exp(col))
                ealasts.append(jnp.exp(alast))
            lhs = jnp.concatenate([jnp.concatenate(m_parts, axis=1), jnp.concatenate(s_parts, axis=1)],
                                  axis=0).astype(BF16)
            xp = conv_ref[:, ps]
            rhs = jnp.concatenate([jnp.where(left, xp, 0.0), jnp.where(left, 0.0, xp)], axis=0).astype(BF16)
            res = jnp.dot(lhs, rhs, preferred_element_type=F32)
            st_old = state_ref[:, ps]
            y_off = jnp.dot(c_gb, st_old.astype(BF16), preferred_element_type=F32)
            y_ref[:, ps] = res[0:q] + y_off * jnp.where(left, ecols[0], ecols[1]) + dskip_ref[:, ps] * xp
            state_ref[:, ps] = st_old * jnp.where(left, ealasts[0], ealasts[1]) + res[q:2 * q]
        return c

    lax.fori_loop(0, SSD_GROUPS, group_body, 0, unroll=4)

    gw = SSD_INNER // SSD_GROUPS

    def norm_body(g, c):
        gs = pl.ds(pl.multiple_of(g * gw, gw), gw)
        yg = y_ref[:, gs] * _silu(z_ref[:, gs].astype(F32))
        o_ref[:, gs] = (_rms_scale(yg) * normw_ref[:, gs]).astype(o_ref.dtype)
        return c

    lax.fori_loop(0, SSD_GROUPS, norm_body, 0, unroll=4)


def _ssd(pack, dt_raw, conv_w, conv_b, dt_bias, a_log, d_skip, norm_w, batch, seq):
    t = pack.shape[0]
    nc = seq // SSD_CHUNK
    q = SSD_CHUNK
    pad = LANES - SSD_HEADS
    row = lambda b, c: b * nc + c
    const = lambda shape: pl.BlockSpec(shape, lambda b, c: (0, 0))
    return pl.pallas_call(
        _ssd_kernel,
        grid=(batch, nc),
        in_specs=[
            pl.BlockSpec((q, SSD_CONV_CH), lambda b, c: (row(b, c), 0)),
            pl.BlockSpec((q, SSD_INNER), lambda b, c: (row(b, c), PACK_Z // 2)),
            pl.BlockSpec((q, LANES), lambda b, c: (row(b, c), 0)),
            const((SSD_CONV, SSD_CONV_CH)), const((1, SSD_CONV_CH)), const((1, LANES)), const((1, LANES)),
            const((1, SSD_INNER)), const((1, SSD_INNER)),
        ],
        out_specs=pl.BlockSpec((q, SSD_INNER), lambda b, c: (row(b, c), 0)),
        out_shape=jax.ShapeDtypeStruct((t, SSD_INNER), BF16),
        scratch_shapes=[
            pltpu.VMEM((2 * q, SSD_CONV_CH), BF16),
            pltpu.VMEM((SSD_STATE, SSD_INNER), F32),
            pltpu.VMEM((q, SSD_CONV_CH), F32),
            pltpu.VMEM((q, SSD_INNER), F32),
            pltpu.VMEM((SSD_GROUPS, q, LANES), F32),
            pltpu.VMEM((LANES, q), F32),
        ],
        compiler_params=_cparams("parallel", "arbitrary"),
        name="ssd",
    )(pack, pack, dt_raw, conv_w, conv_b.reshape(1, -1),
      jnp.pad(dt_bias, (0, pad)).reshape(1, LANES), jnp.pad(a_log, (0, pad)).reshape(1, LANES),
      jnp.repeat(d_skip, SSD_HEAD_DIM).reshape(1, SSD_INNER), norm_w.reshape(1, SSD_INNER))


ATTN_HB = 4


def _attn_kernel(q_ref, k_ref, v_ref, o_ref, lse_ref, *scratch, dilation):
    assert dilation in (1, 4, 16)
    blk = ATTN_BLOCK
    seq = q_ref.shape[1]
    length = seq // dilation
    nblk = length // blk
    quarter = seq // 4
    direct = dilation == 1
    if not direct:
        nat_ref, tmp_ref, qrm_ref, krm_ref, vrm_ref, orm_ref, lrm_ref = scratch
    hsl = lambda h: slice(h * LANES, (h + 1) * LANES)

    def split_rows(src_ref, dst_ref, h):
        nat_ref[...] = src_ref[0, :, hsl(h)].astype(F32)
        if dilation == 4:
            for r in range(4):
                dst_ref[r * quarter:(r + 1) * quarter, hsl(h)] = (
                    nat_ref[pl.ds(r, quarter, stride=4), :].astype(BF16))
        else:
            for r in range(4):
                tmp_ref[r * quarter:(r + 1) * quarter, :] = nat_ref[pl.ds(r, quarter, stride=4), :]
            for r in range(4):
                for r2 in range(4):
                    c = r * 4 + r2
                    dst_ref[c * length:(c + 1) * length, hsl(h)] = (
                        tmp_ref[pl.ds(r * quarter + r2, length, stride=4), :].astype(BF16))

    def merge_rows(src):
        if dilation == 4:
            for r in range(4):
                nat_ref[pl.ds(r, quarter, stride=4), :] = src[r * quarter:(r + 1) * quarter, :]
        else:
            for r in range(4):
                for r2 in range(4):
                    c = r * 4 + r2
                    tmp_ref[pl.ds(r * quarter + r2, length, stride=4), :] = src[c * length:(c + 1) * length, :]
            for r in range(4):
                nat_ref[pl.ds(r, quarter, stride=4), :] = tmp_ref[r * quarter:(r + 1) * quarter, :]

    if direct:
        src = dict(q=q_ref, k=k_ref, v=v_ref)
        rows = lambda name, h, r0, n: src[name][0, pl.ds(r0, n), hsl(h)]
    else:
        for h in range(ATTN_HB):
            split_rows(q_ref, qrm_ref, h)
            split_rows(k_ref, krm_ref, h)
            split_rows(v_ref, vrm_ref, h)
        krm_ref[seq:seq + blk, :] = jnp.zeros((blk, krm_ref.shape[1]), BF16)
        vrm_ref[seq:seq + blk, :] = jnp.zeros((blk, vrm_ref.shape[1]), BF16)
        src = dict(q=qrm_ref, k=krm_ref, v=vrm_ref)
        rows = lambda name, h, r0, n: src[name][pl.ds(r0, n), hsl(h)]

    qi = lax.broadcasted_iota(jnp.int32, (blk, 2 * blk), 0)
    kj = lax.broadcasted_iota(jnp.int32, (blk, 2 * blk), 1)
    lane = lax.broadcasted_iota(jnp.int32, (blk, LANES), 1)
    nt = (((1,), (1,)), ((), ()))

    def block(i, c):
        r0 = pl.multiple_of(i * blk, blk)
        delta = jnp.where(i % nblk == 0, 0, blk)
        k0 = pl.multiple_of(r0 - delta, blk)
        mask = (kj <= qi + delta) & (kj >= qi + (delta - blk))
        lse_tile = jnp.zeros((blk, LANES), F32)
        for h in range(ATTN_HB):
            s = lax.dot_general(rows("q", h, r0, blk), rows("k", h, k0, 2 * blk), nt, preferred_element_type=F32)
            s = jnp.where(mask, s, NEG_BIG)
            m = jnp.max(s, axis=-1, keepdims=True)
            p = jnp.exp(s - m).astype(BF16)
            vx = jnp.concatenate([rows("v", h, k0, 2 * blk), jnp.ones((2 * blk, LANES), BF16)], axis=1)
            r = jnp.dot(p, vx, preferred_element_type=F32)
            l = r[:, LANES:]
            o = r[:, :LANES] * (1.0 / l)
            if direct:
                o_ref[0, pl.ds(r0, blk), hsl(h)] = o.astype(o_ref.dtype)
            else:
                orm_ref[h, pl.ds(r0, blk), :] = o
            lse_tile = jnp.where(lane == h, m + jnp.log(l), lse_tile)
        if direct:
            lse_ref[0, pl.ds(r0, blk), :] = lse_tile
        else:
            lrm_ref[pl.ds(r0, blk), :] = lse_tile
        return c

    lax.fori_loop(0, seq // blk, block, 0, unroll=4)

    if not direct:
        for h in range(ATTN_HB):
            merge_rows(orm_ref.at[h])
            o_ref[0, :, hsl(h)] = nat_ref[...].astype(o_ref.dtype)
        merge_rows(lrm_ref)
        lse_ref[0] = nat_ref[...]


def _attention_group(pack3, g, dilation):
    batch, seq, _ = pack3.shape
    halves = ATTN_HEADS // ATTN_HB
    wb = ATTN_HB * ATTN_HEAD_DIM
    col = lambda unit: (lambda b, hh: (b, 0, unit * halves + hh))
    scratch = [] if dilation == 1 else [
        pltpu.VMEM((seq, LANES), F32),
        pltpu.VMEM((seq, LANES), F32),
        pltpu.VMEM((seq, wb), BF16),
        pltpu.VMEM((seq + ATTN_BLOCK, wb), BF16),
        pltpu.VMEM((seq + ATTN_BLOCK, wb), BF16),
        pltpu.VMEM((ATTN_HB, seq, LANES), F32),
        pltpu.VMEM((seq, LANES), F32),
    ]
    out, lse = pl.pallas_call(
        functools.partial(_attn_kernel, dilation=dilation),
        grid=(batch, halves),
        in_specs=[
            pl.BlockSpec((1, seq, wb), col(PACK_Q + g)),
            pl.BlockSpec((1, seq, wb), col(PACK_K)),
            pl.BlockSpec((1, seq, wb), col(PACK_V)),
        ],
        out_specs=[
            pl.BlockSpec((1, seq, wb), lambda b, hh: (b, 0, hh)),
            pl.BlockSpec((1, seq, LANES), lambda b, hh: (b, 0, hh)),
        ],
        out_shape=[
            jax.ShapeDtypeStruct((batch, seq, ATTN_OUT), BF16),
            jax.ShapeDtypeStruct((batch, seq, halves * LANES), F32),
        ],
        scratch_shapes=scratch,
        compiler_params=_cparams("parallel", "parallel"),
        name=f"attn_d{dilation}",
    )(pack3, pack3, pack3)
    return out.reshape(batch * seq, ATTN_OUT), lse.reshape(batch * seq, halves * LANES)


def _rope_tables(seq):
    half = ROPE_DIM // 2
    inv = ROPE_THETA ** (-jnp.arange(0, ROPE_DIM, 2, dtype=F32) / ROPE_DIM)
    ang = jnp.arange(seq, dtype=F32)[:, None] * inv[None, :]
    c, s = jnp.cos(ang), jnp.sin(ang)
    gap = LANES // 2 - half
    one, zero = jnp.ones((seq, gap), F32), jnp.zeros((seq, gap), F32)
    cos = jnp.concatenate([c, one, c, one], axis=1)
    sin = jnp.concatenate([-s, zero, s, zero], axis=1)
    return cos, sin


def _rope_layout(w):
    half = ROPE_DIM // 2
    d, n = w.shape
    w = w.reshape(d, n // ATTN_HEAD_DIM, ATTN_HEAD_DIM)
    mid = LANES // 2
    w = jnp.concatenate([w[..., :half], w[..., mid:mid + half], w[..., 2 * half:mid], w[..., half:2 * half],
                         w[..., mid + half:]], axis=-1)
    return w.reshape(d, n)


def _merge_kernel(yssd_ref, o0_ref, o1_ref, o2_ref, l0_ref, l1_ref, l2_ref, gs_ref, ga_ref, h_ref,
                  wa_ref, wb_ref, wo_ref, nw_ref, hout_ref, u_ref, yattn_ref):
    l0, l1, l2 = l0_ref[...], l1_ref[...], l2_ref[...]
    m = jnp.maximum(jnp.maximum(l0, l1), l2)
    e0, e1, e2 = jnp.exp(l0 - m), jnp.exp(l1 - m), jnp.exp(l2 - m)
    inv = 1.0 / (e0 + e1 + e2)
    w0, w1, w2 = e0 * inv, e1 * inv, e2 * inv
    for h in range(ATTN_HEADS):
        sl = slice(h * LANES, (h + 1) * LANES)
        c = (h // ATTN_HB) * LANES + h % ATTN_HB
        yh = (w0[:, c:c + 1] * o0_ref[:, sl].astype(F32) + w1[:, c:c + 1] * o1_ref[:, sl].astype(F32)
              + w2[:, c:c + 1] * o2_ref[:, sl].astype(F32))
        yattn_ref[:, sl] = yh.astype(BF16)
    a = jnp.dot(yssd_ref[...], wa_ref[...], preferred_element_type=F32)
    b = jnp.dot(yattn_ref[...], wb_ref[...], preferred_element_type=F32)
    merged = _sigmoid(gs_ref[...].astype(F32)) * a + _sigmoid(ga_ref[...].astype(F32)) * b
    hn = h_ref[...] + jnp.dot(merged.astype(BF16), wo_ref[...], preferred_element_type=F32)
    hout_ref[...] = hn
    u_ref[...] = (_rms_scale(hn) * nw_ref[...]).astype(u_ref.dtype)


def _merge(y_ssd, outs, lses, pack, h, w_a, w_b, w_o, norm_w, tm=256):
    t = h.shape[0]
    d = D_MODEL
    rows = lambda w, col=0: pl.BlockSpec((tm, w), lambda i, col=col: (i, col))
    whole = lambda a: pl.BlockSpec(a.shape, lambda i: (0, 0))
    nw = norm_w.reshape(1, d)
    lw = lses[0].shape[1]
    return pl.pallas_call(
        _merge_kernel,
        grid=(t // tm,),
        in_specs=[rows(SSD_INNER), rows(d), rows(d), rows(d), rows(lw), rows(lw), rows(lw),
                  rows(d, PACK_GS), rows(d, PACK_GA), rows(d), whole(w_a), whole(w_b), whole(w_o), whole(nw)],
        out_specs=[rows(d), rows(d)],
        out_shape=[jax.ShapeDtypeStruct((t, d), F32), jax.ShapeDtypeStruct((t, d), BF16)],
        scratch_shapes=[pltpu.VMEM((tm, ATTN_OUT), BF16)],
        compiler_params=_cparams("parallel"),
        name="merge",
    )(y_ssd, *outs, *lses, pack, pack, h, w_a, w_b, w_o, nw)


def _ffn_kernel(u_ref, wg_ref, wu_ref, wd_ref, h_ref, nw_ref, *rest, last):
    if last:
        out_ref, acc_ref = rest
    else:
        hout_ref, u_out_ref, acc_ref = rest
    j = pl.program_id(1)

    @pl.when(j == 0)
    def _():
        acc_ref[...] = h_ref[...]

    u = u_ref[...]
    gate = jnp.dot(u, wg_ref[...], preferred_element_type=F32)
    up = jnp.dot(u, wu_ref[...], preferred_element_type=F32)
    act = (_silu(gate) * up).astype(BF16)
    acc_ref[...] += jnp.dot(act, wd_ref[...], preferred_element_type=F32)

    @pl.when(j == pl.num_programs(1) - 1)
    def _():
        hn = acc_ref[...]
        normed = _rms_scale(hn) * nw_ref[...]
        if last:
            out_ref[...] = normed
        else:
            hout_ref[...] = hn
            u_out_ref[...] = normed.astype(u_out_ref.dtype)


def _ffn(u, w_gu, w_d, h, norm_w, last, tm=1024, th=256):
    t, d = u.shape
    nh = FFN_HIDDEN // th
    rows = lambda: pl.BlockSpec((tm, d), lambda i, j: (i, 0))
    if last:
        out_specs = rows()
        out_shape = jax.ShapeDtypeStruct((t, d), F32)
    else:
        out_specs = [rows(), rows()]
        out_shape = [jax.ShapeDtypeStruct((t, d), F32), jax.ShapeDtypeStruct((t, d), BF16)]
    return pl.pallas_call(
        functools.partial(_ffn_kernel, last=last),
        grid=(t // tm, nh),
        in_specs=[
            rows(),
            pl.BlockSpec((d, th), lambda i, j: (0, j)),
            pl.BlockSpec((d, th), lambda i, j: (0, nh + j)),
            pl.BlockSpec((th, d), lambda i, j: (j, 0)),
            rows(),
            pl.BlockSpec((1, d), lambda i, j: (0, 0)),
        ],
        out_specs=out_specs,
        out_shape=out_shape,
        scratch_shapes=[pltpu.VMEM((tm, d), F32)],
        compiler_params=_cparams("parallel", "arbitrary"),
        name="ffn_last" if last else "ffn",
    )(u, w_gu, w_gu, w_d, h, norm_w.reshape(1, d))


def _pack_in_weights(w):
    z0, xbc0, dt0, q0, k0, v0, gs0, ga0, end = 0, 2048, 5120, 5152, 8224, 9248, 10272, 11296, 12320
    packed = jnp.concatenate([w[:, xbc0:dt0], _rope_layout(w[:, q0:k0]), w[:, z0:xbc0], _rope_layout(w[:, k0:v0]),
                              w[:, v0:end]], axis=1)
    w_dt = jnp.pad(w[:, dt0:q0], ((0, 0), (0, LANES - SSD_HEADS)))
    return packed.astype(BF16), w_dt.astype(BF16)


def kernel(x, norm_mix, w_in, conv_w, conv_b, dt_bias, a_log, d_skip, ssd_norm, w_ssd_branch, w_attn_branch,
           w_out, norm_ffn, w_gate_up, w_down, norm_final):
    batch, seq, d = x.shape
    depth = w_in.shape[0]
    t = batch * seq
    tables = _rope_tables(seq)
    h = x.reshape(t, d)
    u = _rmsnorm(h, norm_mix[0])
    for layer in range(depth):
        w_pack, w_dt = _pack_in_weights(w_in[layer])
        pack, dt_raw = _inproj(u, w_pack, w_dt, tables)
        y_ssd = _ssd(pack, dt_raw, conv_w[layer], conv_b[layer], dt_bias[layer], a_log[layer], d_skip[layer],
                     ssd_norm[layer], batch, seq)
        pack3 = pack.reshape(batch, seq, PACK_W)
        outs, lses = [], []
        for g, (window, dilation) in enumerate(ATTN_PATTERNS):
            assert window // dilation == ATTN_BLOCK
            o, l = _attention_group(pack3, g, dilation)
            outs.append(o)
            lses.append(l)
        h, u_ffn = _merge(y_ssd, outs, lses, pack, h, w_ssd_branch[layer].astype(BF16),
                          w_attn_branch[layer].astype(BF16), w_out[layer].astype(BF16), norm_ffn[layer])
        last = layer == depth - 1
        nxt = norm_final if last else norm_mix[layer + 1]
        res = _ffn(u_ffn, w_gate_up[layer].astype(BF16), w_down[layer].astype(BF16), h, nxt, last)
        if last:
            return res.reshape(batch, seq, d)
        h, u = res
```

```python
import functools
import math

import jax
import jax.numpy as jnp
from jax import lax
from jax.experimental import pallas as pl
from jax.experimental.pallas import tpu as pltpu

F32 = jnp.float32
BF16 = jnp.bfloat16

D_MODEL = 1024
RMS_EPS = 1e-5

SSD_INNER = 2048
SSD_HEAD_DIM = 64
SSD_HEADS = 32
SSD_STATE = 128
SSD_GROUPS = 4
SSD_CONV = 4
SSD_CHUNK = 128
SSD_BC = 2 * SSD_GROUPS * SSD_STATE
SSD_CONV_CH = SSD_INNER + SSD_BC

ATTN_HEAD_DIM = 128
ATTN_HEADS = 8
ATTN_PATTERNS = ((128, 1), (512, 4), (2048, 16))
ATTN_BLOCK = 128
ROPE_THETA = 500000.0
ROPE_DIM = 32
ATTN_OUT = ATTN_HEADS * ATTN_HEAD_DIM

FFN_HIDDEN = 2816

PACK_W = 12 * 1024
PACK_Q, PACK_Z, PACK_K, PACK_V, PACK_GS, PACK_GA = 3, 6, 8, 9, 10, 11

Q_SCALE = ATTN_HEAD_DIM ** -0.5 * math.log2(math.e)
LN2 = math.log(2.0)

LANES = 128
NEG_BIG = -1e30
VMEM_LIMIT = 56 * 1024 * 1024


def _cparams(*sem):
    return pltpu.CompilerParams(dimension_semantics=sem, vmem_limit_bytes=VMEM_LIMIT)


def _sigmoid(x):
    return 1.0 / (1.0 + jnp.exp(-x))


def _silu(x):
    h = 0.5 * x
    return h + h * jnp.tanh(h)


def _softplus(x):
    return jnp.maximum(x, 0.0) + jnp.log(1.0 + jnp.exp(-jnp.abs(x)))


def _rms_scale(x):
    return x * lax.rsqrt(jnp.mean(x * x, axis=-1, keepdims=True) + RMS_EPS)


def _rmsnorm_kernel(x_ref, w_ref, o_ref):
    o_ref[...] = (_rms_scale(x_ref[...]) * w_ref[...]).astype(o_ref.dtype)


def _rmsnorm(x, w, tm=1024):
    t, d = x.shape
    return pl.pallas_call(
        _rmsnorm_kernel,
        grid=(t // tm,),
        in_specs=[pl.BlockSpec((tm, d), lambda i: (i, 0)), pl.BlockSpec((1, d), lambda i: (0, 0))],
        out_specs=pl.BlockSpec((tm, d), lambda i: (i, 0)),
        out_shape=jax.ShapeDtypeStruct((t, d), BF16),
        compiler_params=_cparams("parallel"),
        name="rmsnorm",
    )(x, w.reshape(1, d))


IN_DT_COL = SSD_INNER + SSD_CONV_CH


def _pack_kernel(a_ref, b_ref, o_ref, dt_ref):
    j = pl.program_id(0)
    rows = a_ref.shape[1]
    groups = a_ref.shape[2] // LANES
    lane = lax.broadcasted_iota(jnp.int32, (rows, LANES), 1)
    late = (j >= PACK_Q) & (j < PACK_Z) | (j >= PACK_K)
    rotary = (j >= PACK_Q) & (j < PACK_Z) | (j == PACK_K)
    half, mid = ROPE_DIM // 2, LANES // 2
    dt_w = SSD_HEADS

    def shifted(g):
        cur = a_ref[0, :, g * LANES:(g + 1) * LANES]
        nxt = a_ref[0, :, (g + 1) * LANES:(g + 2) * LANES] if g + 1 < groups else b_ref[0]
        return jnp.where(lane < LANES - dt_w, pltpu.roll(cur, LANES - dt_w, 1), pltpu.roll(nxt, LANES - dt_w, 1))

    @pl.when(jnp.logical_not(late))
    def _():
        o_ref[...] = a_ref[0].astype(o_ref.dtype)

    @pl.when(late & jnp.logical_not(rotary))
    def _():
        for g in range(groups):
            o_ref[:, g * LANES:(g + 1) * LANES] = shifted(g).astype(o_ref.dtype)

    @pl.when(rotary)
    def _():
        for g in range(groups):
            s = shifted(g)
            s = jnp.where((lane >= half) & (lane < 2 * half), pltpu.roll(s, LANES - (mid - half), 1),
                          jnp.where((lane >= mid) & (lane < mid + half), pltpu.roll(s, mid - half, 1), s))
            o_ref[:, g * LANES:(g + 1) * LANES] = s.astype(o_ref.dtype)

    @pl.when(j == PACK_Q)
    def _():
        dt_ref[...] = jnp.where(lane < dt_w, a_ref[0, :, 0:LANES], 0.0).astype(dt_ref.dtype)


def _pack_in_weights(w_in, layer):
    _, d, _ = w_in.shape
    unit = ATTN_OUT
    first = IN_DT_COL // unit
    src = lambda j: jnp.where(j < PACK_Q, j + SSD_INNER // unit,
                              jnp.where(j < PACK_Z, j - PACK_Q + first, jnp.where(j < PACK_K, j - PACK_Z, j)))
    return pl.pallas_call(
        _pack_kernel,
        grid=(PACK_W // unit,),
        in_specs=[
            pl.BlockSpec((1, d, unit), lambda j: (layer, 0, src(j))),
            pl.BlockSpec((1, d, LANES), lambda j: (layer, 0, (src(j) + 1) * (unit // LANES))),
        ],
        out_specs=[pl.BlockSpec((d, unit), lambda j: (0, j)), pl.BlockSpec((d, LANES), lambda j: (0, 0))],
        out_shape=[jax.ShapeDtypeStruct((d, PACK_W), BF16), jax.ShapeDtypeStruct((d, LANES), BF16)],
        compiler_params=_cparams("arbitrary"),
        name="pack_w_in",
    )(w_in, w_in)


def _inproj_kernel(u_ref, w_ref, wdt_ref, cos_ref, sin_ref, o_ref, dt_ref):
    j = pl.program_id(1)
    is_q = (j >= PACK_Q) & (j < PACK_Q + len(ATTN_PATTERNS))
    rotary = is_q | (j == PACK_K)

    @pl.when(rotary)
    def _():
        res = jnp.dot(u_ref[...], w_ref[...], preferred_element_type=F32)
        scale = jnp.where(is_q, Q_SCALE, 1.0)
        cos, sin = cos_ref[...] * scale, sin_ref[...] * scale
        for h in range(res.shape[1] // LANES):
            x = res[:, h * LANES:(h + 1) * LANES]
            o_ref[:, h * LANES:(h + 1) * LANES] = (x * cos + pltpu.roll(x, LANES // 2, 1) * sin).astype(o_ref.dtype)

    @pl.when(jnp.logical_not(rotary))
    def _():
        o_ref[...] = jnp.dot(u_ref[...], w_ref[...], preferred_element_type=F32).astype(o_ref.dtype)

    @pl.when(j == 0)
    def _():
        dt_ref[...] = jnp.dot(u_ref[...], wdt_ref[...], preferred_element_type=F32)


def _inproj(u, w_pack, w_dt, tables, tm=2048):
    t, d = u.shape
    n = w_pack.shape[1]
    tn = ATTN_OUT
    seq = tables[0].shape[0]
    tab_spec = pl.BlockSpec((tm, LANES), lambda i, j: (i % (seq // tm), 0))
    return pl.pallas_call(
        _inproj_kernel,
        grid=(t // tm, n // tn),
        in_specs=[
            pl.BlockSpec((tm, d), lambda i, j: (i, 0)),
            pl.BlockSpec((d, tn), lambda i, j: (0, j)),
            pl.BlockSpec((d, LANES), lambda i, j: (0, 0)),
            tab_spec, tab_spec,
        ],
        out_specs=[
            pl.BlockSpec((tm, tn), lambda i, j: (i, j)),
            pl.BlockSpec((tm, LANES), lambda i, j: (i, 0)),
        ],
        out_shape=[jax.ShapeDtypeStruct((t, n), BF16), jax.ShapeDtypeStruct((t, LANES), F32)],
        compiler_params=_cparams("parallel", "arbitrary"),
        name="inproj",
    )(u, w_pack, w_dt, *tables)


def _ssd_kernel(xbc_ref, z_ref, dt_ref, convw_ref, convb_ref, dtb_ref, alog_ref, dskip_ref, normw_ref,
                o_ref, xbuf_ref, state_ref, conv_ref, y_ref, acum_ref, rowm_ref):
    q = SSD_CHUNK
    hpg = SSD_HEADS // SSD_GROUPS

    @pl.when(pl.program_id(1) == 0)
    def _():
        xbuf_ref[0:q, :] = jnp.zeros((q, SSD_CONV_CH), BF16)
        state_ref[...] = jnp.zeros_like(state_ref)

    xbuf_ref[q:2 * q, :] = xbc_ref[...]
    ti = lax.broadcasted_iota(jnp.int32, (3 * q, 2 * q), 0)
    si = lax.broadcasted_iota(jnp.int32, (3 * q, 2 * q), 1)
    tap = lax.shift_right_logical(ti, q.bit_length() - 1)
    shift = jnp.where(si == (ti & (q - 1)) + (q - 3) + tap, 1.0, 0.0).astype(BF16)
    cw = 512

    def conv_body(i, c):
        cs = pl.ds(pl.multiple_of(i * cw, cw), cw)
        xx = xbuf_ref[:, cs]
        sh = jnp.dot(shift, xx, preferred_element_type=F32)
        w = convw_ref[:, cs]
        acc = (convb_ref[:, cs] + w[3:4] * xx[q:2 * q].astype(F32) + w[0:1] * sh[0:q] + w[1:2] * sh[q:2 * q]
               + w[2:3] * sh[2 * q:3 * q])
        conv_ref[:, cs] = _silu(acc)
        return c

    lax.fori_loop(0, SSD_CONV_CH // cw, conv_body, 0, unroll=3)
    xbuf_ref[0:q, :] = xbuf_ref[q:2 * q, :]

    dt = _softplus(dt_ref[...] + dtb_ref[...])
    adt = dt * (-jnp.exp(alog_ref[...]))
    rowi = lax.broadcasted_iota(jnp.int32, (q, q), 0)
    coli = lax.broadcasted_iota(jnp.int32, (q, q), 1)
    acum = adt
    for s in (1, 2, 4, 8, 16, 32, 64):
        acum = acum + jnp.where(rowi >= s, pltpu.roll(acum, s, 0), 0.0)
    rowm_ref[...] = acum.T - jnp.log(dt.T)
    for g in range(SSD_GROUPS):
        acum_ref[g] = pltpu.roll(acum, (LANES - hpg * g) % LANES, 1)
    causal = rowi >= coli
    left = coli < SSD_HEAD_DIM

    def group_body(g, c):
        a_g = acum_ref[g]
        rm_g = rowm_ref[pl.ds(pl.multiple_of(g * hpg, hpg), hpg), :]
        b_g = conv_ref[:, pl.ds(pl.multiple_of(SSD_INNER + g * SSD_STATE, LANES), SSD_STATE)]
        c_g = conv_ref[:, pl.ds(pl.multiple_of(SSD_INNER + (SSD_GROUPS + g) * SSD_STATE, LANES), SSD_STATE)]
        c_gb = c_g.astype(BF16)
        cb = lax.dot_general(c_gb, b_g.astype(BF16), (((1,), (1,)), ((), ())), preferred_element_type=F32)
        b_t = b_g.T
        for pp in range(hpg // 2):
            ps = pl.ds(pl.multiple_of((g * (hpg // 2) + pp) * LANES, LANES), LANES)
            m_parts, s_parts, ecols, ealasts = [], [], [], []
            for j in (2 * pp, 2 * pp + 1):
                col = a_g[:, j:j + 1]
                rowm = rm_g[j:j + 1, :]
                alast = a_g[q - 1:q, j:j + 1]
                m_parts.append(cb * jnp.exp(jnp.where(causal, col - rowm, NEG_BIG)))
                s_parts.append(b_t * jnp.exp(alast - rowm))
                ecols.append(jnp.exp(col))
                ealasts.append(jnp.exp(alast))
            lhs = jnp.concatenate([jnp.concatenate(m_parts, axis=1), jnp.concatenate(s_parts, axis=1)],
                                  axis=0).astype(BF16)
            xp = conv_ref[:, ps]
            rhs = jnp.concatenate([jnp.where(left, xp, 0.0), jnp.where(left, 0.0, xp)], axis=0).astype(BF16)
            res = jnp.dot(lhs, rhs, preferred_element_type=F32)
            st_old = state_ref[:, ps]
            y_off = jnp.dot(c_gb, st_old.astype(BF16), preferred_element_type=F32)
            y_ref[:, ps] = res[0:q] + y_off * jnp.where(left, ecols[0], ecols[1]) + dskip_ref[:, ps] * xp
            state_ref[:, ps] = st_old * jnp.where(left, ealasts[0], ealasts[1]) + res[q:2 * q]
        return c

    lax.fori_loop(0, SSD_GROUPS, group_body, 0, unroll=4)

    gw = SSD_INNER // SSD_GROUPS

    def norm_body(g, c):
        gs = pl.ds(pl.multiple_of(g * gw, gw), gw)
        yg = y_ref[:, gs] * _silu(z_ref[:, gs].astype(F32))
        o_ref[:, gs] = (_rms_scale(yg) * normw_ref[:, gs]).astype(o_ref.dtype)
        return c

    lax.fori_loop(0, SSD_GROUPS, norm_body, 0, unroll=4)


def _ssd(pack, dt_raw, conv_w, conv_b, dt_bias, a_log, d_skip, norm_w, batch, seq):
    t = pack.shape[0]
    nc = seq // SSD_CHUNK
    q = SSD_CHUNK
    pad = LANES - SSD_HEADS
    row = lambda b, c: b * nc + c
    const = lambda shape: pl.BlockSpec(shape, lambda b, c: (0, 0))
    return pl.pallas_call(
        _ssd_kernel,
        grid=(batch, nc),
        in_specs=[
            pl.BlockSpec((q, SSD_CONV_CH), lambda b, c: (row(b, c), 0)),
            pl.BlockSpec((q, SSD_INNER), lambda b, c: (row(b, c), PACK_Z // 2)),
            pl.BlockSpec((q, LANES), lambda b, c: (row(b, c), 0)),
            const((SSD_CONV, SSD_CONV_CH)), const((1, SSD_CONV_CH)), const((1, LANES)), const((1, LANES)),
            const((1, SSD_INNER)), const((1, SSD_INNER)),
        ],
        out_specs=pl.BlockSpec((q, SSD_INNER), lambda b, c: (row(b, c), 0)),
        out_shape=jax.ShapeDtypeStruct((t, SSD_INNER), BF16),
        scratch_shapes=[
            pltpu.VMEM((2 * q, SSD_CONV_CH), BF16),
            pltpu.VMEM((SSD_STATE, SSD_INNER), F32),
            pltpu.VMEM((q, SSD_CONV_CH), F32),
            pltpu.VMEM((q, SSD_INNER), F32),
            pltpu.VMEM((SSD_GROUPS, q, LANES), F32),
            pltpu.VMEM((LANES, q), F32),
        ],
        compiler_params=_cparams("parallel", "arbitrary"),
        name="ssd",
    )(pack, pack, dt_raw, conv_w, conv_b.reshape(1, -1),
      jnp.pad(dt_bias, (0, pad)).reshape(1, LANES), jnp.pad(a_log, (0, pad)).reshape(1, LANES),
      jnp.repeat(d_skip, SSD_HEAD_DIM).reshape(1, SSD_INNER), norm_w.reshape(1, SSD_INNER))


ATTN_HB = 4


def _attn_kernel(q_ref, k_ref, v_ref, o_ref, lse_ref, bias_ref, *scratch, dilation):
    assert dilation in (1, 4, 16)
    blk = ATTN_BLOCK
    seq = q_ref.shape[1]
    length = seq // dilation
    nblk = length // blk
    quarter = seq // 4
    direct = dilation == 1
    if not direct:
        nat_ref, tmp_ref, qrm_ref, krm_ref, vrm_ref, orm_ref, lrm_ref = scratch
    hsl = lambda h: slice(h * LANES, (h + 1) * LANES)

    def split_rows(src_ref, dst_ref, h):
        nat_ref[...] = src_ref[0, :, hsl(h)].astype(F32)
        if dilation == 4:
            for r in range(4):
                dst_ref[r * quarter:(r + 1) * quarter, hsl(h)] = (
                    nat_ref[pl.ds(r, quarter, stride=4), :].astype(BF16))
        else:
            for r in range(4):
                tmp_ref[r * quarter:(r + 1) * quarter, :] = nat_ref[pl.ds(r, quarter, stride=4), :]
            for r in range(4):
                for r2 in range(4):
                    c = r * 4 + r2
                    dst_ref[c * length:(c + 1) * length, hsl(h)] = (
                        tmp_ref[pl.ds(r * quarter + r2, length, stride=4), :].astype(BF16))

    def merge_rows(src):
        if dilation == 4:
            for r in range(4):
                nat_ref[pl.ds(r, quarter, stride=4), :] = src[r * quarter:(r + 1) * quarter, :]
        else:
            for r in range(4):
                for r2 in range(4):
                    c = r * 4 + r2
                    tmp_ref[pl.ds(r * quarter + r2, length, stride=4), :] = src[c * length:(c + 1) * length, :]
            for r in range(4):
                nat_ref[pl.ds(r, quarter, stride=4), :] = tmp_ref[r * quarter:(r + 1) * quarter, :]

    if direct:
        src = dict(q=q_ref, k=k_ref, v=v_ref)
        rows = lambda name, h, r0, n: src[name][0, pl.ds(r0, n), hsl(h)]
    else:
        for h in range(ATTN_HB):
            split_rows(q_ref, qrm_ref, h)
            split_rows(k_ref, krm_ref, h)
            split_rows(v_ref, vrm_ref, h)
        krm_ref[seq:seq + blk, :] = jnp.zeros((blk, krm_ref.shape[1]), BF16)
        vrm_ref[seq:seq + blk, :] = jnp.zeros((blk, vrm_ref.shape[1]), BF16)
        src = dict(q=qrm_ref, k=krm_ref, v=vrm_ref)
        rows = lambda name, h, r0, n: src[name][pl.ds(r0, n), hsl(h)]

    qi = lax.broadcasted_iota(jnp.int32, (blk, 2 * blk), 0)
    kj = lax.broadcasted_iota(jnp.int32, (blk, 2 * blk), 1)
    lane = lax.broadcasted_iota(jnp.int32, (blk, LANES), 1)
    nt = (((1,), (1,)), ((), ()))
    bias_ref[0] = jnp.where(kj <= qi, 0.0, NEG_BIG)
    bias_ref[1] = jnp.where((kj >= qi) & (kj <= qi + blk), 0.0, NEG_BIG)

    def block(i, c):
        r0 = pl.multiple_of(i * blk, blk)
        later = jnp.where(i % nblk == 0, 0, 1)
        k0 = pl.multiple_of(r0 - later * blk, blk)
        bias = bias_ref[later]
        lse_tile = jnp.zeros((blk, LANES), F32)
        for h in range(ATTN_HB):
            s = lax.dot_general(rows("q", h, r0, blk), rows("k", h, k0, 2 * blk), nt, preferred_element_type=F32)
            s = s + bias
            m = jnp.max(s, axis=-1, keepdims=True)
            p = jnp.exp2(s - m).astype(BF16)
            vx = jnp.concatenate([rows("v", h, k0, 2 * blk), jnp.ones((2 * blk, LANES), BF16)], axis=1)
            r = jnp.dot(p, vx, preferred_element_type=F32)
            l = r[:, LANES:]
            o = r[:, :LANES] * (1.0 / l)
            if direct:
                o_ref[0, pl.ds(r0, blk), hsl(h)] = o.astype(o_ref.dtype)
            else:
                orm_ref[h, pl.ds(r0, blk), :] = o
            lse_tile = jnp.where(lane == h, m + jnp.log(l) * (1.0 / LN2), lse_tile)
        if direct:
            lse_ref[0, pl.ds(r0, blk), :] = lse_tile
        else:
            lrm_ref[pl.ds(r0, blk), :] = lse_tile
        return c

    lax.fori_loop(0, seq // blk, block, 0, unroll=4)

    if not direct:
        for h in range(ATTN_HB):
            merge_rows(orm_ref.at[h])
            o_ref[0, :, hsl(h)] = nat_ref[...].astype(o_ref.dtype)
        merge_rows(lrm_ref)
        lse_ref[0] = nat_ref[...]


def _attention_group(pack3, g, dilation):
    batch, seq, _ = pack3.shape
    halves = ATTN_HEADS // ATTN_HB
    wb = ATTN_HB * ATTN_HEAD_DIM
    col = lambda unit: (lambda b, hh: (b, 0, unit * halves + hh))
    scratch = [] if dilation == 1 else [
        pltpu.VMEM((seq, LANES), F32),
        pltpu.VMEM((seq, LANES), F32),
        pltpu.VMEM((seq, wb), BF16),
        pltpu.VMEM((seq + ATTN_BLOCK, wb), BF16),
        pltpu.VMEM((seq + ATTN_BLOCK, wb), BF16),
        pltpu.VMEM((ATTN_HB, seq, LANES), F32),
        pltpu.VMEM((seq, LANES), F32),
    ]
    out, lse = pl.pallas_call(
        functools.partial(_attn_kernel, dilation=dilation),
        grid=(batch, halves),
        in_specs=[
            pl.BlockSpec((1, seq, wb), col(PACK_Q + g)),
            pl.BlockSpec((1, seq, wb), col(PACK_K)),
            pl.BlockSpec((1, seq, wb), col(PACK_V)),
        ],
        out_specs=[
            pl.BlockSpec((1, seq, wb), lambda b, hh: (b, 0, hh)),
            pl.BlockSpec((1, seq, LANES), lambda b, hh: (b, 0, hh)),
        ],
        out_shape=[
            jax.ShapeDtypeStruct((batch, seq, ATTN_OUT), BF16),
            jax.ShapeDtypeStruct((batch, seq, halves * LANES), F32),
        ],
        scratch_shapes=[pltpu.VMEM((2, ATTN_BLOCK, 2 * ATTN_BLOCK), F32)] + scratch,
        compiler_params=_cparams("parallel", "parallel"),
        name=f"attn_d{dilation}",
    )(pack3, pack3, pack3)
    return out.reshape(batch * seq, ATTN_OUT), lse.reshape(batch * seq, halves * LANES)


def _rope_tables(seq):
    half = ROPE_DIM // 2
    inv = ROPE_THETA ** (-jnp.arange(0, ROPE_DIM, 2, dtype=F32) / ROPE_DIM)
    ang = jnp.arange(seq, dtype=F32)[:, None] * inv[None, :]
    c, s = jnp.cos(ang), jnp.sin(ang)
    gap = LANES // 2 - half
    one, zero = jnp.ones((seq, gap), F32), jnp.zeros((seq, gap), F32)
    cos = jnp.concatenate([c, one, c, one], axis=1)
    sin = jnp.concatenate([-s, zero, s, zero], axis=1)
    return cos, sin


def _merge_kernel(yssd_ref, o0_ref, o1_ref, o2_ref, l0_ref, l1_ref, l2_ref, gs_ref, ga_ref, h_ref,
                  wa_ref, wb_ref, wo_ref, nw_ref, hout_ref, u_ref, yattn_ref):
    l0, l1, l2 = l0_ref[...], l1_ref[...], l2_ref[...]
    m = jnp.maximum(jnp.maximum(l0, l1), l2)
    e0, e1, e2 = jnp.exp2(l0 - m), jnp.exp2(l1 - m), jnp.exp2(l2 - m)
    inv = 1.0 / (e0 + e1 + e2)
    w0, w1, w2 = e0 * inv, e1 * inv, e2 * inv
    for h in range(ATTN_HEADS):
        sl = slice(h * LANES, (h + 1) * LANES)
        c = (h // ATTN_HB) * LANES + h % ATTN_HB
        yh = (w0[:, c:c + 1] * o0_ref[:, sl].astype(F32) + w1[:, c:c + 1] * o1_ref[:, sl].astype(F32)
              + w2[:, c:c + 1] * o2_ref[:, sl].astype(F32))
        yattn_ref[:, sl] = yh.astype(BF16)
    a = jnp.dot(yssd_ref[...], wa_ref[...], preferred_element_type=F32)
    b = jnp.dot(yattn_ref[...], wb_ref[...], preferred_element_type=F32)
    merged = _sigmoid(gs_ref[...].astype(F32)) * a + _sigmoid(ga_ref[...].astype(F32)) * b
    hn = h_ref[...] + jnp.dot(merged.astype(BF16), wo_ref[...], preferred_element_type=F32)
    hout_ref[...] = hn
    u_ref[...] = (_rms_scale(hn) * nw_ref[...]).astype(u_ref.dtype)


def _resident(a):
    return pl.BlockSpec(a.shape, lambda *_: (0,) * a.ndim, pipeline_mode=pl.Buffered(1))


def _merge(y_ssd, outs, lses, pack, h, w_a, w_b, w_o, norm_w, tm=512):
    t = h.shape[0]
    d = D_MODEL
    rows = lambda w, col=0: pl.BlockSpec((tm, w), lambda i, col=col: (i, col))
    nw = norm_w.reshape(1, d)
    lw = lses[0].shape[1]
    return pl.pallas_call(
        _merge_kernel,
        grid=(t // tm,),
        in_specs=[rows(SSD_INNER), rows(d), rows(d), rows(d), rows(lw), rows(lw), rows(lw),
                  rows(d, PACK_GS), rows(d, PACK_GA), rows(d), _resident(w_a), _resident(w_b), _resident(w_o),
                  _resident(nw)],
        out_specs=[rows(d), rows(d)],
        out_shape=[jax.ShapeDtypeStruct((t, d), F32), jax.ShapeDtypeStruct((t, d), BF16)],
        scratch_shapes=[pltpu.VMEM((tm, ATTN_OUT), BF16)],
        compiler_params=_cparams("parallel"),
        name="merge",
    )(y_ssd, *outs, *lses, pack, pack, h, w_a, w_b, w_o, nw)


FFN_TH = 256


def _ffn_kernel(u_ref, wgu_ref, wd_ref, h_ref, nw_ref, *rest, last):
    if last:
        out_ref, acc_ref = rest
    else:
        hout_ref, u_out_ref, acc_ref = rest
    acc_ref[...] = h_ref[...]
    u = u_ref[...]

    def hidden_step(j, c):
        lo = pl.multiple_of(j * FFN_TH, FFN_TH)
        gate = jnp.dot(u, wgu_ref[:, pl.ds(lo, FFN_TH)], preferred_element_type=F32)
        up = jnp.dot(u, wgu_ref[:, pl.ds(lo + FFN_HIDDEN, FFN_TH)], preferred_element_type=F32)
        act = (_silu(gate) * up).astype(BF16)
        acc_ref[...] += jnp.dot(act, wd_ref[pl.ds(lo, FFN_TH), :], preferred_element_type=F32)
        return c

    lax.fori_loop(0, FFN_HIDDEN // FFN_TH, hidden_step, 0)
    hn = acc_ref[...]
    normed = _rms_scale(hn) * nw_ref[...]
    if last:
        out_ref[...] = normed
    else:
        hout_ref[...] = hn
        u_out_ref[...] = normed.astype(u_out_ref.dtype)


def _ffn(u, w_gu, w_d, h, norm_w, last, tm=1024):
    t, d = u.shape
    rows = lambda: pl.BlockSpec((tm, d), lambda i: (i, 0))
    nw = norm_w.reshape(1, d)
    if last:
        out_specs = rows()
        out_shape = jax.ShapeDtypeStruct((t, d), F32)
    else:
        out_specs = [rows(), rows()]
        out_shape = [jax.ShapeDtypeStruct((t, d), F32), jax.ShapeDtypeStruct((t, d), BF16)]
    return pl.pallas_call(
        functools.partial(_ffn_kernel, last=last),
        grid=(t // tm,),
        in_specs=[rows(), _resident(w_gu), _resident(w_d), rows(), _resident(nw)],
        out_specs=out_specs,
        out_shape=out_shape,
        scratch_shapes=[pltpu.VMEM((tm, d), F32)],
        compiler_params=_cparams("parallel"),
        name="ffn_last" if last else "ffn",
    )(u, w_gu, w_d, h, nw)


def kernel(x, norm_mix, w_in, conv_w, conv_b, dt_bias, a_log, d_skip, ssd_norm, w_ssd_branch, w_attn_branch,
           w_out, norm_ffn, w_gate_up, w_down, norm_final):
    batch, seq, d = x.shape
    depth = w_in.shape[0]
    t = batch * seq
    tables = _rope_tables(seq)
    h = x.reshape(t, d)
    u = _rmsnorm(h, norm_mix[0])
    for layer in range(depth):
        w_pack, w_dt = _pack_in_weights(w_in, layer)
        pack, dt_raw = _inproj(u, w_pack, w_dt, tables)
        y_ssd = _ssd(pack, dt_raw, conv_w[layer], conv_b[layer], dt_bias[layer], a_log[layer], d_skip[layer],
                     ssd_norm[layer], batch, seq)
        pack3 = pack.reshape(batch, seq, PACK_W)
        outs, lses = [], []
        for g, (window, dilation) in enumerate(ATTN_PATTERNS):
            assert window // dilation == ATTN_BLOCK
            o, l = _attention_group(pack3, g, dilation)
            outs.append(o)
            lses.append(l)
        h, u_ffn = _merge(y_ssd, outs, lses, pack, h, w_ssd_branch[layer].astype(BF16),
                          w_attn_branch[layer].astype(BF16), w_out[layer].astype(BF16), norm_ffn[layer])
        last = layer == depth - 1
        nxt = norm_final if last else norm_mix[layer + 1]
        res = _ffn(u_ffn, w_gate_up[layer].astype(BF16), w_down[layer].astype(BF16), h, nxt, last)
        if last:
            return res.reshape(batch, seq, d)
        h, u = res
```

```python
import functools
import math

import jax
import jax.numpy as jnp
from jax import lax
from jax.experimental import pallas as pl
from jax.experimental.pallas import tpu as pltpu

F32 = jnp.float32
BF16 = jnp.bfloat16

D_MODEL = 1024
RMS_EPS = 1e-5

SSD_INNER = 2048
SSD_HEAD_DIM = 64
SSD_HEADS = 32
SSD_STATE = 128
SSD_GROUPS = 4
SSD_CONV = 4
SSD_CHUNK = 128
SSD_BC = 2 * SSD_GROUPS * SSD_STATE
SSD_CONV_CH = SSD_INNER + SSD_BC

ATTN_HEAD_DIM = 128
ATTN_HEADS = 8
ATTN_PATTERNS = ((128, 1), (512, 4), (2048, 16))
ATTN_BLOCK = 128
ROPE_THETA = 500000.0
ROPE_DIM = 32
ATTN_OUT = ATTN_HEADS * ATTN_HEAD_DIM

FFN_HIDDEN = 2816

PACK_W = 12 * 1024
PACK_Q, PACK_Z, PACK_K, PACK_V, PACK_GS, PACK_GA = 3, 6, 8, 9, 10, 11

Q_SCALE = ATTN_HEAD_DIM ** -0.5 * math.log2(math.e)
LN2 = math.log(2.0)

LANES = 128
NEG_BIG = -1e30
VMEM_LIMIT = 56 * 1024 * 1024


def _cparams(*sem):
    return pltpu.CompilerParams(dimension_semantics=sem, vmem_limit_bytes=VMEM_LIMIT)


def _sigmoid(x):
    return 1.0 / (1.0 + jnp.exp(-x))


def _silu(x):
    h = 0.5 * x
    return h + h * jnp.tanh(h)


def _softplus(x):
    return jnp.maximum(x, 0.0) + jnp.log(1.0 + jnp.exp(-jnp.abs(x)))


def _rms_scale(x):
    return x * lax.rsqrt(jnp.mean(x * x, axis=-1, keepdims=True) + RMS_EPS)


def _rmsnorm_kernel(x_ref, w_ref, o_ref):
    o_ref[...] = (_rms_scale(x_ref[...]) * w_ref[...]).astype(o_ref.dtype)


def _rmsnorm(x, w, tm=1024):
    t, d = x.shape
    return pl.pallas_call(
        _rmsnorm_kernel,
        grid=(t // tm,),
        in_specs=[pl.BlockSpec((tm, d), lambda i: (i, 0)), pl.BlockSpec((1, d), lambda i: (0, 0))],
        out_specs=pl.BlockSpec((tm, d), lambda i: (i, 0)),
        out_shape=jax.ShapeDtypeStruct((t, d), BF16),
        compiler_params=_cparams("parallel"),
        name="rmsnorm",
    )(x, w.reshape(1, d))


IN_DT_COL = SSD_INNER + SSD_CONV_CH
NT_DIMS = (((1,), (1,)), ((), ()))


def _pack_kernel(a_ref, o_ref):
    j = pl.program_id(0)
    rotary = (j >= PACK_Q) & (j < PACK_Z) | (j == PACK_K)
    half, mid = ROPE_DIM // 2, ATTN_HEAD_DIM // 2

    @pl.when(jnp.logical_not(rotary))
    def _():
        o_ref[...] = a_ref[0].astype(o_ref.dtype)

    @pl.when(rotary)
    def _():
        for h in range(a_ref.shape[1] // ATTN_HEAD_DIM):
            b = h * ATTN_HEAD_DIM
            for dst, src, n in ((0, 0, half), (half, mid, half), (2 * half, 2 * half, mid - 2 * half),
                                (mid, half, half), (mid + half, mid + half, mid - half)):
                o_ref[b + dst:b + dst + n, :] = a_ref[0, b + src:b + src + n, :].astype(o_ref.dtype)


def _pack_in_weights(w_in, layer):
    _, d, _ = w_in.shape
    unit = ATTN_OUT
    w_t = jnp.swapaxes(w_in, 1, 2)
    late = IN_DT_COL + SSD_HEADS
    g = SSD_HEADS
    ug, lg = unit // g, late // g
    src = lambda j: g * jnp.where(j < PACK_Q, SSD_INNER // g + j * ug,
                                  jnp.where(j < PACK_Z, lg + (j - PACK_Q) * ug,
                                            jnp.where(j < PACK_K, (j - PACK_Z) * ug, lg + (j - PACK_Q - 2) * ug)))
    packed = pl.pallas_call(
        _pack_kernel,
        grid=(PACK_W // unit,),
        in_specs=[pl.BlockSpec((pl.Element(1), pl.Element(unit), pl.Element(d)), lambda j: (layer, src(j), 0))],
        out_specs=pl.BlockSpec((unit, d), lambda j: (j, 0)),
        out_shape=jax.ShapeDtypeStruct((PACK_W, d), BF16),
        compiler_params=_cparams("arbitrary"),
        name="pack_w_in",
    )(w_t)
    w_dt = jnp.pad(w_t[layer, IN_DT_COL:late, :], ((0, LANES - SSD_HEADS), (0, 0)))
    return packed, w_dt


def _inproj_kernel(u_ref, w_ref, wdt_ref, cos_ref, sin_ref, o_ref, dt_ref):
    j = pl.program_id(1)
    is_q = (j >= PACK_Q) & (j < PACK_Q + len(ATTN_PATTERNS))
    rotary = is_q | (j == PACK_K)

    @pl.when(rotary)
    def _():
        res = lax.dot_general(u_ref[...], w_ref[...], NT_DIMS, preferred_element_type=F32)
        scale = jnp.where(is_q, Q_SCALE, 1.0)
        cos, sin = cos_ref[...] * scale, sin_ref[...] * scale
        for h in range(res.shape[1] // LANES):
            x = res[:, h * LANES:(h + 1) * LANES]
            o_ref[:, h * LANES:(h + 1) * LANES] = (x * cos + pltpu.roll(x, LANES // 2, 1) * sin).astype(o_ref.dtype)

    @pl.when(jnp.logical_not(rotary))
    def _():
        o_ref[...] = lax.dot_general(u_ref[...], w_ref[...], NT_DIMS, preferred_element_type=F32).astype(o_ref.dtype)

    @pl.when(j == 0)
    def _():
        dt_ref[...] = lax.dot_general(u_ref[...], wdt_ref[...].astype(BF16), NT_DIMS, preferred_element_type=F32)


def _inproj(u, w_pack, w_dt, tables, tm=2048):
    t, d = u.shape
    n = w_pack.shape[0]
    tn = ATTN_OUT
    seq = tables[0].shape[0]
    tab_spec = pl.BlockSpec((tm, LANES), lambda i, j: (i % (seq // tm), 0))
    return pl.pallas_call(
        _inproj_kernel,
        grid=(t // tm, n // tn),
        in_specs=[
            pl.BlockSpec((tm, d), lambda i, j: (i, 0)),
            pl.BlockSpec((tn, d), lambda i, j: (j, 0)),
            pl.BlockSpec((LANES, d), lambda i, j: (0, 0)),
            tab_spec, tab_spec,
        ],
        out_specs=[
            pl.BlockSpec((tm, tn), lambda i, j: (i, j)),
            pl.BlockSpec((tm, LANES), lambda i, j: (i, 0)),
        ],
        out_shape=[jax.ShapeDtypeStruct((t, n), BF16), jax.ShapeDtypeStruct((t, LANES), F32)],
        compiler_params=_cparams("parallel", "arbitrary"),
        name="inproj",
    )(u, w_pack, w_dt, *tables)


SSD_NB = 2


def _ssd_kernel(xbc_ref, z_ref, dt_ref, *refs):
    convw_ref, convb_ref, dtb_ref, alog_ref, dskip_ref, normw_ref = refs[:6]
    o_ref = refs[6]
    xbuf_ref, state_ref, conv_ref, y_ref, acum_ref, rowm_ref = refs[7:]
    for s in range(SSD_NB):
        _ssd_conv(xbc_ref.at[0, s], convw_ref, convb_ref, xbuf_ref.at[s], state_ref.at[s], conv_ref.at[s])
    for s in range(SSD_NB):
        _ssd_scan(z_ref.at[0, s], dt_ref.at[0, s], dtb_ref, alog_ref, dskip_ref, normw_ref, o_ref.at[0, s],
                  state_ref.at[s], conv_ref.at[s], y_ref.at[s], acum_ref.at[s], rowm_ref.at[s])


def _ssd_conv(xbc_ref, convw_ref, convb_ref, xbuf_ref, state_ref, conv_ref):
    q = SSD_CHUNK

    @pl.when(pl.program_id(1) == 0)
    def _():
        xbuf_ref[0:q, :] = jnp.zeros((q, SSD_CONV_CH), BF16)
        state_ref[...] = jnp.zeros_like(state_ref)

    xbuf_ref[q:2 * q, :] = xbc_ref[...]
    ti = lax.broadcasted_iota(jnp.int32, (3 * q, 2 * q), 0)
    si = lax.broadcasted_iota(jnp.int32, (3 * q, 2 * q), 1)
    tap = lax.shift_right_logical(ti, q.bit_length() - 1)
    shift = jnp.where(si == (ti & (q - 1)) + (q - 3) + tap, 1.0, 0.0).astype(BF16)
    cw = 512

    def conv_body(i, c):
        cs = pl.ds(pl.multiple_of(i * cw, cw), cw)
        xx = xbuf_ref[:, cs]
        sh = jnp.dot(shift, xx, preferred_element_type=F32)
        w = convw_ref[:, cs]
        acc = (convb_ref[:, cs] + w[3:4] * xx[q:2 * q].astype(F32) + w[0:1] * sh[0:q] + w[1:2] * sh[q:2 * q]
               + w[2:3] * sh[2 * q:3 * q])
        conv_ref[:, cs] = _silu(acc)
        return c

    lax.fori_loop(0, SSD_CONV_CH // cw, conv_body, 0, unroll=3)
    xbuf_ref[0:q, :] = xbuf_ref[q:2 * q, :]


def _ssd_scan(z_ref, dt_ref, dtb_ref, alog_ref, dskip_ref, normw_ref, o_ref, state_ref, conv_ref, y_ref,
              acum_ref, rowm_ref):
    q = SSD_CHUNK
    hpg = SSD_HEADS // SSD_GROUPS

    dt = _softplus(dt_ref[...] + dtb_ref[...])
    adt = dt * (-jnp.exp(alog_ref[...]))
    rowi = lax.broadcasted_iota(jnp.int32, (q, q), 0)
    coli = lax.broadcasted_iota(jnp.int32, (q, q), 1)
    acum = adt
    for s in (1, 2, 4, 8, 16, 32, 64):
        acum = acum + jnp.where(rowi >= s, pltpu.roll(acum, s, 0), 0.0)
    rowm_ref[...] = acum.T - jnp.log(dt.T)
    for g in range(SSD_GROUPS):
        acum_ref[g] = pltpu.roll(acum, (LANES - hpg * g) % LANES, 1)
    causal = rowi >= coli
    left = coli < SSD_HEAD_DIM

    def group_body(g, c):
        a_g = acum_ref[g]
        rm_g = rowm_ref[pl.ds(pl.multiple_of(g * hpg, hpg), hpg), :]
        b_g = conv_ref[:, pl.ds(pl.multiple_of(SSD_INNER + g * SSD_STATE, LANES), SSD_STATE)]
        c_g = conv_ref[:, pl.ds(pl.multiple_of(SSD_INNER + (SSD_GROUPS + g) * SSD_STATE, LANES), SSD_STATE)]
        c_gb = c_g.astype(BF16)
        cb = lax.dot_general(c_gb, b_g.astype(BF16), (((1,), (1,)), ((), ())), preferred_element_type=F32)
        b_t = b_g.T
        for pp in range(hpg // 2):
            ps = pl.ds(pl.multiple_of((g * (hpg // 2) + pp) * LANES, LANES), LANES)
            m_parts, s_parts, ecols, ealasts = [], [], [], []
            for j in (2 * pp, 2 * pp + 1):
                col = a_g[:, j:j + 1]
                rowm = rm_g[j:j + 1, :]
                alast = a_g[q - 1:q, j:j + 1]
                m_parts.append(cb * jnp.exp(jnp.where(causal, col - rowm, NEG_BIG)))
                s_parts.append(b_t * jnp.exp(alast - rowm))
                ecols.append(jnp.exp(col))
                ealasts.append(jnp.exp(alast))
            lhs = jnp.concatenate([jnp.concatenate(m_parts, axis=1), jnp.concatenate(s_parts, axis=1)],
                                  axis=0).astype(BF16)
            xp = conv_ref[:, ps]
            rhs = jnp.concatenate([jnp.where(left, xp, 0.0), jnp.where(left, 0.0, xp)], axis=0).astype(BF16)
            res = jnp.dot(lhs, rhs, preferred_element_type=F32)
            st_old = state_ref[:, ps]
            y_off = jnp.dot(c_gb, st_old.astype(BF16), preferred_element_type=F32)
            y_ref[:, ps] = res[0:q] + y_off * jnp.where(left, ecols[0], ecols[1]) + dskip_ref[:, ps] * xp
            state_ref[:, ps] = st_old * jnp.where(left, ealasts[0], ealasts[1]) + res[q:2 * q]
        return c

    lax.fori_loop(0, SSD_GROUPS, group_body, 0, unroll=4)

    gw = SSD_INNER // SSD_GROUPS

    def norm_body(g, c):
        gs = pl.ds(pl.multiple_of(g * gw, gw), gw)
        yg = y_ref[:, gs] * _silu(z_ref[:, gs].astype(F32))
        o_ref[:, gs] = (_rms_scale(yg) * normw_ref[:, gs]).astype(o_ref.dtype)
        return c

    lax.fori_loop(0, SSD_GROUPS, norm_body, 0, unroll=4)


def _ssd(pack, dt_raw, conv_w, conv_b, dt_bias, a_log, d_skip, norm_w, batch, seq):
    t = pack.shape[0]
    nc = seq // SSD_CHUNK
    q = SSD_CHUNK
    pad = LANES - SSD_HEADS
    nb = SSD_NB
    const = lambda shape: pl.BlockSpec(shape, lambda b, c: (0, 0))
    rows = lambda width, col: pl.BlockSpec((1, nb, q, width), lambda b, c: (b, 0, c, col))
    pack4 = pack.reshape(batch // nb, nb, seq, PACK_W)
    out = pl.pallas_call(
        _ssd_kernel,
        grid=(batch // nb, nc),
        in_specs=[
            rows(SSD_CONV_CH, 0), rows(SSD_INNER, PACK_Z // 2), rows(LANES, 0),
            const((SSD_CONV, SSD_CONV_CH)), const((1, SSD_CONV_CH)), const((1, LANES)), const((1, LANES)),
            const((1, SSD_INNER)), const((1, SSD_INNER)),
        ],
        out_specs=rows(SSD_INNER, 0),
        out_shape=jax.ShapeDtypeStruct((batch // nb, nb, seq, SSD_INNER), BF16),
        scratch_shapes=[
            pltpu.VMEM((nb, 2 * q, SSD_CONV_CH), BF16),
            pltpu.VMEM((nb, SSD_STATE, SSD_INNER), F32),
            pltpu.VMEM((nb, q, SSD_CONV_CH), F32),
            pltpu.VMEM((nb, q, SSD_INNER), F32),
            pltpu.VMEM((nb, SSD_GROUPS, q, LANES), F32),
            pltpu.VMEM((nb, LANES, q), F32),
        ],
        compiler_params=_cparams("parallel", "arbitrary"),
        name="ssd",
    )(pack4, pack4, dt_raw.reshape(batch // nb, nb, seq, LANES), conv_w, conv_b.reshape(1, -1),
      jnp.pad(dt_bias, (0, pad)).reshape(1, LANES), jnp.pad(a_log, (0, pad)).reshape(1, LANES),
      jnp.repeat(d_skip, SSD_HEAD_DIM).reshape(1, SSD_INNER), norm_w.reshape(1, SSD_INNER))
    return out.reshape(t, SSD_INNER)


ATTN_HB = 4


def _attn_kernel(q_ref, k_ref, v_ref, o_ref, lse_ref, bias_ref, *scratch, dilation):
    assert dilation in (1, 4, 16)
    blk = ATTN_BLOCK
    seq = q_ref.shape[1]
    length = seq // dilation
    nblk = length // blk
    quarter = seq // 4
    direct = dilation == 1
    if not direct:
        nat_ref, tmp_ref, qrm_ref, krm_ref, vrm_ref, orm_ref, lrm_ref = scratch
    hsl = lambda h: slice(h * LANES, (h + 1) * LANES)

    def split_rows(src_ref, dst_ref, h):
        nat_ref[...] = src_ref[0, :, hsl(h)].astype(F32)
        if dilation == 4:
            for r in range(4):
                dst_ref[r * quarter:(r + 1) * quarter, hsl(h)] = (
                    nat_ref[pl.ds(r, quarter, stride=4), :].astype(BF16))
        else:
            for r in range(4):
                tmp_ref[r * quarter:(r + 1) * quarter, :] = nat_ref[pl.ds(r, quarter, stride=4), :]
            for r in range(4):
                for r2 in range(4):
                    c = r * 4 + r2
                    dst_ref[c * length:(c + 1) * length, hsl(h)] = (
                        tmp_ref[pl.ds(r * quarter + r2, length, stride=4), :].astype(BF16))

    def merge_rows(src):
        if dilation == 4:
            for r in range(4):
                nat_ref[pl.ds(r, quarter, stride=4), :] = src[r * quarter:(r + 1) * quarter, :]
        else:
            for r in range(4):
                for r2 in range(4):
                    c = r * 4 + r2
                    tmp_ref[pl.ds(r * quarter + r2, length, stride=4), :] = src[c * length:(c + 1) * length, :]
            for r in range(4):
                nat_ref[pl.ds(r, quarter, stride=4), :] = tmp_ref[r * quarter:(r + 1) * quarter, :]

    if direct:
        src = dict(q=q_ref, k=k_ref, v=v_ref)
        rows = lambda name, h, r0, n: src[name][0, pl.ds(r0, n), hsl(h)]
    else:
        for h in range(ATTN_HB):
            split_rows(q_ref, qrm_ref, h)
            split_rows(k_ref, krm_ref, h)
            split_rows(v_ref, vrm_ref, h)
        krm_ref[seq:seq + blk, :] = jnp.zeros((blk, krm_ref.shape[1]), BF16)
        vrm_ref[seq:seq + blk, :] = jnp.zeros((blk, vrm_ref.shape[1]), BF16)
        src = dict(q=qrm_ref, k=krm_ref, v=vrm_ref)
        rows = lambda name, h, r0, n: src[name][pl.ds(r0, n), hsl(h)]

    qi = lax.broadcasted_iota(jnp.int32, (blk, 2 * blk), 0)
    kj = lax.broadcasted_iota(jnp.int32, (blk, 2 * blk), 1)
    lane = lax.broadcasted_iota(jnp.int32, (blk, LANES), 1)
    nt = (((1,), (1,)), ((), ()))
    bias_ref[0] = jnp.where(kj <= qi, 0.0, NEG_BIG)
    bias_ref[1] = jnp.where((kj >= qi) & (kj <= qi + blk), 0.0, NEG_BIG)

    def block(i, c):
        r0 = pl.multiple_of(i * blk, blk)
        later = jnp.where(i % nblk == 0, 0, 1)
        k0 = pl.multiple_of(r0 - later * blk, blk)
        bias = bias_ref[later]
        lse_tile = jnp.zeros((blk, LANES), F32)
        for h in range(ATTN_HB):
            s = lax.dot_general(rows("q", h, r0, blk), rows("k", h, k0, 2 * blk), nt, preferred_element_type=F32)
            s = s + bias
            m = jnp.max(s, axis=-1, keepdims=True)
            p = jnp.exp2(s - m).astype(BF16)
            vx = jnp.concatenate([rows("v", h, k0, 2 * blk), jnp.ones((2 * blk, LANES), BF16)], axis=1)
            r = jnp.dot(p, vx, preferred_element_type=F32)
            l = r[:, LANES:]
            o = r[:, :LANES] * (1.0 / l)
            if direct:
                o_ref[0, pl.ds(r0, blk), hsl(h)] = o.astype(o_ref.dtype)
            else:
                orm_ref[h, pl.ds(r0, blk), :] = o
            lse_tile = jnp.where(lane == h, m + jnp.log(l) * (1.0 / LN2), lse_tile)
        if direct:
            lse_ref[0, pl.ds(r0, blk), :] = lse_tile
        else:
            lrm_ref[pl.ds(r0, blk), :] = lse_tile
        return c

    lax.fori_loop(0, seq // blk, block, 0, unroll=4)

    if not direct:
        for h in range(ATTN_HB):
            merge_rows(orm_ref.at[h])
            o_ref[0, :, hsl(h)] = nat_ref[...].astype(o_ref.dtype)
        merge_rows(lrm_ref)
        lse_ref[0] = nat_ref[...]


def _attention_group(pack3, g, dilation):
    batch, seq, _ = pack3.shape
    halves = ATTN_HEADS // ATTN_HB
    wb = ATTN_HB * ATTN_HEAD_DIM
    col = lambda unit: (lambda b, hh: (b, 0, unit * halves + hh))
    scratch = [] if dilation == 1 else [
        pltpu.VMEM((seq, LANES), F32),
        pltpu.VMEM((seq, LANES), F32),
        pltpu.VMEM((seq, wb), BF16),
        pltpu.VMEM((seq + ATTN_BLOCK, wb), BF16),
        pltpu.VMEM((seq + ATTN_BLOCK, wb), BF16),
        pltpu.VMEM((ATTN_HB, seq, LANES), F32),
        pltpu.VMEM((seq, LANES), F32),
    ]
    out, lse = pl.pallas_call(
        functools.partial(_attn_kernel, dilation=dilation),
        grid=(batch, halves),
        in_specs=[
            pl.BlockSpec((1, seq, wb), col(PACK_Q + g)),
            pl.BlockSpec((1, seq, wb), col(PACK_K)),
            pl.BlockSpec((1, seq, wb), col(PACK_V)),
        ],
        out_specs=[
            pl.BlockSpec((1, seq, wb), lambda b, hh: (b, 0, hh)),
            pl.BlockSpec((1, seq, LANES), lambda b, hh: (b, 0, hh)),
        ],
        out_shape=[
            jax.ShapeDtypeStruct((batch, seq, ATTN_OUT), BF16),
            jax.ShapeDtypeStruct((batch, seq, halves * LANES), F32),
        ],
        scratch_shapes=[pltpu.VMEM((2, ATTN_BLOCK, 2 * ATTN_BLOCK), F32)] + scratch,
        compiler_params=_cparams("parallel", "parallel"),
        name=f"attn_d{dilation}",
    )(pack3, pack3, pack3)
    return out.reshape(batch * seq, ATTN_OUT), lse.reshape(batch * seq, halves * LANES)


def _rope_tables(seq):
    half = ROPE_DIM // 2
    inv = ROPE_THETA ** (-jnp.arange(0, ROPE_DIM, 2, dtype=F32) / ROPE_DIM)
    ang = jnp.arange(seq, dtype=F32)[:, None] * inv[None, :]
    c, s = jnp.cos(ang), jnp.sin(ang)
    gap = LANES // 2 - half
    one, zero = jnp.ones((seq, gap), F32), jnp.zeros((seq, gap), F32)
    cos = jnp.concatenate([c, one, c, one], axis=1)
    sin = jnp.concatenate([-s, zero, s, zero], axis=1)
    return cos, sin


def _merge_kernel(yssd_ref, o0_ref, o1_ref, o2_ref, l0_ref, l1_ref, l2_ref, gs_ref, ga_ref, h_ref,
                  wa_ref, wb_ref, wo_ref, nw_ref, hout_ref, u_ref, yattn_ref):
    l0, l1, l2 = l0_ref[...], l1_ref[...], l2_ref[...]
    m = jnp.maximum(jnp.maximum(l0, l1), l2)
    e0, e1, e2 = jnp.exp2(l0 - m), jnp.exp2(l1 - m), jnp.exp2(l2 - m)
    inv = 1.0 / (e0 + e1 + e2)
    w0, w1, w2 = e0 * inv, e1 * inv, e2 * inv
    for h in range(ATTN_HEADS):
        sl = slice(h * LANES, (h + 1) * LANES)
        c = (h // ATTN_HB) * LANES + h % ATTN_HB
        yh = (w0[:, c:c + 1] * o0_ref[:, sl].astype(F32) + w1[:, c:c + 1] * o1_ref[:, sl].astype(F32)
              + w2[:, c:c + 1] * o2_ref[:, sl].astype(F32))
        yattn_ref[:, sl] = yh.astype(BF16)
    a = jnp.dot(yssd_ref[...], wa_ref[...], preferred_element_type=F32)
    b = jnp.dot(yattn_ref[...], wb_ref[...], preferred_element_type=F32)
    merged = _sigmoid(gs_ref[...].astype(F32)) * a + _sigmoid(ga_ref[...].astype(F32)) * b
    hn = h_ref[...] + jnp.dot(merged.astype(BF16), wo_ref[...], preferred_element_type=F32)
    hout_ref[...] = hn
    u_ref[...] = (_rms_scale(hn) * nw_ref[...]).astype(u_ref.dtype)


def _resident(a):
    return pl.BlockSpec(a.shape, lambda *_: (0,) * a.ndim, pipeline_mode=pl.Buffered(1))


def _merge(y_ssd, outs, lses, pack, h, w_a, w_b, w_o, norm_w, tm=512):
    t = h.shape[0]
    d = D_MODEL
    rows = lambda w, col=0: pl.BlockSpec((tm, w), lambda i, col=col: (i, col))
    nw = norm_w.reshape(1, d)
    lw = lses[0].shape[1]
    return pl.pallas_call(
        _merge_kernel,
        grid=(t // tm,),
        in_specs=[rows(SSD_INNER), rows(d), rows(d), rows(d), rows(lw), rows(lw), rows(lw),
                  rows(d, PACK_GS), rows(d, PACK_GA), rows(d), _resident(w_a), _resident(w_b), _resident(w_o),
                  _resident(nw)],
        out_specs=[rows(d), rows(d)],
        out_shape=[jax.ShapeDtypeStruct((t, d), F32), jax.ShapeDtypeStruct((t, d), BF16)],
        scratch_shapes=[pltpu.VMEM((tm, ATTN_OUT), BF16)],
        compiler_params=_cparams("parallel"),
        name="merge",
    )(y_ssd, *outs, *lses, pack, pack, h, w_a, w_b, w_o, nw)


FFN_TH = 256


def _ffn_kernel(u_ref, wgu_ref, wd_ref, h_ref, nw_ref, *rest, last):
    if last:
        out_ref, acc_ref = rest
    else:
        hout_ref, u_out_ref, acc_ref = rest
    acc_ref[...] = h_ref[...]
    u = u_ref[...]

    def hidden_step(j, c):
        lo = pl.multiple_of(j * FFN_TH, FFN_TH)
        gate = jnp.dot(u, wgu_ref[:, pl.ds(lo, FFN_TH)], preferred_element_type=F32)
        up = jnp.dot(u, wgu_ref[:, pl.ds(lo + FFN_HIDDEN, FFN_TH)], preferred_element_type=F32)
        act = (_silu(gate) * up).astype(BF16)
        acc_ref[...] += jnp.dot(act, wd_ref[pl.ds(lo, FFN_TH), :], preferred_element_type=F32)
        return c

    lax.fori_loop(0, FFN_HIDDEN // FFN_TH, hidden_step, 0)
    hn = acc_ref[...]
    normed = _rms_scale(hn) * nw_ref[...]
    if last:
        out_ref[...] = normed
    else:
        hout_ref[...] = hn
        u_out_ref[...] = normed.astype(u_out_ref.dtype)


def _ffn(u, w_gu, w_d, h, norm_w, last, tm=1024):
    t, d = u.shape
    rows = lambda: pl.BlockSpec((tm, d), lambda i: (i, 0))
    nw = norm_w.reshape(1, d)
    if last:
        out_specs = rows()
        out_shape = jax.ShapeDtypeStruct((t, d), F32)
    else:
        out_specs = [rows(), rows()]
        out_shape = [jax.ShapeDtypeStruct((t, d), F32), jax.ShapeDtypeStruct((t, d), BF16)]
    return pl.pallas_call(
        functools.partial(_ffn_kernel, last=last),
        grid=(t // tm,),
        in_specs=[rows(), _resident(w_gu), _resident(w_d), rows(), _resident(nw)],
        out_specs=out_specs,
        out_shape=out_shape,
        scratch_shapes=[pltpu.VMEM((tm, d), F32)],
        compiler_params=_cparams("parallel"),
        name="ffn_last" if last else "ffn",
    )(u, w_gu, w_d, h, nw)


def kernel(x, norm_mix, w_in, conv_w, conv_b, dt_bias, a_log, d_skip, ssd_norm, w_ssd_branch, w_attn_branch,
           w_out, norm_ffn, w_gate_up, w_down, norm_final):
    batch, seq, d = x.shape
    depth = w_in.shape[0]
    t = batch * seq
    tables = _rope_tables(seq)
    h = x.reshape(t, d)
    u = _rmsnorm(h, norm_mix[0])
    for layer in range(depth):
        w_pack, w_dt = _pack_in_weights(w_in, layer)
        pack, dt_raw = _inproj(u, w_pack, w_dt, tables)
        y_ssd = _ssd(pack, dt_raw, conv_w[layer], conv_b[layer], dt_bias[layer], a_log[layer], d_skip[layer],
                     ssd_norm[layer], batch, seq)
        pack3 = pack.reshape(batch, seq, PACK_W)
        outs, lses = [], []
        for g, (window, dilation) in enumerate(ATTN_PATTERNS):
            assert window // dilation == ATTN_BLOCK
            o, l = _attention_group(pack3, g, dilation)
            outs.append(o)
            lses.append(l)
        h, u_ffn = _merge(y_ssd, outs, lses, pack, h, w_ssd_branch[layer].astype(BF16),
                          w_attn_branch[layer].astype(BF16), w_out[layer].astype(BF16), norm_ffn[layer])
        last = layer == depth - 1
        nxt = norm_final if last else norm_mix[layer + 1]
        res = _ffn(u_ffn, w_gate_up[layer].astype(BF16), w_down[layer].astype(BF16), h, nxt, last)
        if last:
            return res.reshape(batch, seq, d)
        h, u = res
```

```python
import functools
import math

import jax
import jax.numpy as jnp
from jax import lax
from jax.experimental import pallas as pl
from jax.experimental.pallas import tpu as pltpu

F32 = jnp.float32
BF16 = jnp.bfloat16

D_MODEL = 1024
RMS_EPS = 1e-5

SSD_INNER = 2048
SSD_HEAD_DIM = 64
SSD_HEADS = 32
SSD_STATE = 128
SSD_GROUPS = 4
SSD_CONV = 4
SSD_CHUNK = 128
SSD_BC = 2 * SSD_GROUPS * SSD_STATE
SSD_CONV_CH = SSD_INNER + SSD_BC

ATTN_HEAD_DIM = 128
ATTN_HEADS = 8
ATTN_PATTERNS = ((128, 1), (512, 4), (2048, 16))
ATTN_BLOCK = 128
ROPE_THETA = 500000.0
ROPE_DIM = 32
ATTN_OUT = ATTN_HEADS * ATTN_HEAD_DIM

FFN_HIDDEN = 2816

PACK_W = 12 * 1024
PACK_Q, PACK_Z, PACK_K, PACK_V, PACK_GS, PACK_GA = 3, 6, 8, 9, 10, 11

Q_SCALE = ATTN_HEAD_DIM ** -0.5 * math.log2(math.e)
LN2 = math.log(2.0)

LANES = 128
NEG_BIG = -1e30
VMEM_LIMIT = 56 * 1024 * 1024


def _cparams(*sem):
    return pltpu.CompilerParams(dimension_semantics=sem, vmem_limit_bytes=VMEM_LIMIT)


def _sigmoid(x):
    return 1.0 / (1.0 + jnp.exp(-x))


def _silu(x):
    h = 0.5 * x
    return h + h * jnp.tanh(h)


def _softplus(x):
    return jnp.maximum(x, 0.0) + jnp.log(1.0 + jnp.exp(-jnp.abs(x)))


def _rms_scale(x):
    return x * lax.rsqrt(jnp.mean(x * x, axis=-1, keepdims=True) + RMS_EPS)


def _rmsnorm_kernel(x_ref, w_ref, o_ref):
    o_ref[...] = (_rms_scale(x_ref[...]) * w_ref[...]).astype(o_ref.dtype)


def _rmsnorm(x, w, tm=1024):
    t, d = x.shape
    return pl.pallas_call(
        _rmsnorm_kernel,
        grid=(t // tm,),
        in_specs=[pl.BlockSpec((tm, d), lambda i: (i, 0)), pl.BlockSpec((1, d), lambda i: (0, 0))],
        out_specs=pl.BlockSpec((tm, d), lambda i: (i, 0)),
        out_shape=jax.ShapeDtypeStruct((t, d), BF16),
        compiler_params=_cparams("parallel"),
        name="rmsnorm",
    )(x, w.reshape(1, d))


IN_DT_COL = SSD_INNER + SSD_CONV_CH
NT_DIMS = (((1,), (1,)), ((), ()))


def _pack_kernel(a_ref, o_ref):
    j = pl.program_id(0)
    rotary = (j >= PACK_Q) & (j < PACK_Z) | (j == PACK_K)
    half, mid = ROPE_DIM // 2, ATTN_HEAD_DIM // 2

    @pl.when(jnp.logical_not(rotary))
    def _():
        o_ref[...] = a_ref[0].astype(o_ref.dtype)

    @pl.when(rotary)
    def _():
        for h in range(a_ref.shape[1] // ATTN_HEAD_DIM):
            b = h * ATTN_HEAD_DIM
            for dst, src, n in ((0, 0, half), (half, mid, half), (2 * half, 2 * half, mid - 2 * half),
                                (mid, half, half), (mid + half, mid + half, mid - half)):
                o_ref[b + dst:b + dst + n, :] = a_ref[0, b + src:b + src + n, :].astype(o_ref.dtype)


def _pack_in_weights(w_in, layer):
    _, d, _ = w_in.shape
    unit = ATTN_OUT
    w_t = jnp.swapaxes(w_in, 1, 2)
    late = IN_DT_COL + SSD_HEADS
    g = SSD_HEADS
    ug, lg = unit // g, late // g
    src = lambda j: g * jnp.where(j < PACK_Q, SSD_INNER // g + j * ug,
                                  jnp.where(j < PACK_Z, lg + (j - PACK_Q) * ug,
                                            jnp.where(j < PACK_K, (j - PACK_Z) * ug, lg + (j - PACK_Q - 2) * ug)))
    packed = pl.pallas_call(
        _pack_kernel,
        grid=(PACK_W // unit,),
        in_specs=[pl.BlockSpec((pl.Element(1), pl.Element(unit), pl.Element(d)), lambda j: (layer, src(j), 0))],
        out_specs=pl.BlockSpec((unit, d), lambda j: (j, 0)),
        out_shape=jax.ShapeDtypeStruct((PACK_W, d), BF16),
        compiler_params=_cparams("arbitrary"),
        name="pack_w_in",
    )(w_t)
    w_dt = jnp.pad(w_t[layer, IN_DT_COL:late, :], ((0, LANES - SSD_HEADS), (0, 0)))
    return packed, w_dt


def _regroup(y, nat_ref, tmp_ref, g4_ref, g16_ref, cols):
    seq = y.shape[0]
    quarter, length = seq // 4, seq // 16
    nat_ref[...] = y
    for r in range(4):
        tmp_ref[r * quarter:(r + 1) * quarter, :] = nat_ref[pl.ds(r, quarter, stride=4), :]
    if g4_ref is not None:
        g4_ref[:, cols] = tmp_ref[...].astype(g4_ref.dtype)
    if g16_ref is not None:
        for r in range(4):
            for r2 in range(4):
                c = r * 4 + r2
                g16_ref[c * length:(c + 1) * length, cols] = (
                    tmp_ref[pl.ds(r * quarter + r2, length, stride=4), :].astype(g16_ref.dtype))


def _inproj_kernel(u_ref, w_ref, wdt_ref, cos_ref, sin_ref, o_ref, dt_ref, g4_ref, g16_ref, nat_ref, tmp_ref):
    j = pl.program_id(1)
    heads = o_ref.shape[1] // LANES

    def unit(rotary, scale, by4, by16):
        res = lax.dot_general(u_ref[...], w_ref[...], NT_DIMS, preferred_element_type=F32)
        if not (rotary or by4 or by16):
            o_ref[...] = res.astype(o_ref.dtype)
            return
        if rotary:
            cos, sin = cos_ref[...] * scale, sin_ref[...] * scale
        for h in range(heads):
            cols = slice(h * LANES, (h + 1) * LANES)
            y = res[:, cols]
            if rotary:
                y = y * cos + pltpu.roll(y, LANES // 2, 1) * sin
            o_ref[:, cols] = y.astype(o_ref.dtype)
            if by4 or by16:
                _regroup(y, nat_ref.at[h % 2], tmp_ref.at[h % 2], g4_ref if by4 else None,
                         g16_ref if by16 else None, cols)

    special = {PACK_Q: (True, Q_SCALE, False, False), PACK_Q + 1: (True, Q_SCALE, True, False),
               PACK_Q + 2: (True, Q_SCALE, False, True), PACK_K: (True, 1.0, True, True),
               PACK_V: (False, 1.0, True, True)}
    plain = True
    for unit_idx, args in special.items():
        pl.when(j == unit_idx)(functools.partial(unit, *args))
        plain = plain & (j != unit_idx)
    pl.when(plain)(functools.partial(unit, False, 1.0, False, False))

    @pl.when(j == 0)
    def _():
        dt_ref[...] = lax.dot_general(u_ref[...], wdt_ref[...].astype(BF16), NT_DIMS, preferred_element_type=F32)


def _inproj(u, w_pack, w_dt, tables, seq):
    t, d = u.shape
    n = w_pack.shape[0]
    tm, tn = seq, ATTN_OUT
    once = dict(pipeline_mode=pl.Buffered(1))
    tab_spec = pl.BlockSpec((tm, LANES), lambda i, j: (0, 0), **once)
    gcol = lambda i, j: (i, jnp.where(j < PACK_K, 0, jnp.where(j == PACK_K, 1, 2)))
    return pl.pallas_call(
        _inproj_kernel,
        grid=(t // tm, n // tn),
        in_specs=[
            pl.BlockSpec((tm, d), lambda i, j: (i, 0), **once),
            pl.BlockSpec((tn, d), lambda i, j: (j, 0)),
            pl.BlockSpec((LANES, d), lambda i, j: (0, 0), **once),
            tab_spec, tab_spec,
        ],
        out_specs=[
            pl.BlockSpec((tm, tn), lambda i, j: (i, j)),
            pl.BlockSpec((tm, LANES), lambda i, j: (i, 0)),
            pl.BlockSpec((tm, tn), gcol),
            pl.BlockSpec((tm, tn), gcol),
        ],
        out_shape=[jax.ShapeDtypeStruct((t, n), BF16), jax.ShapeDtypeStruct((t, LANES), F32),
                   jax.ShapeDtypeStruct((t, 3 * tn), BF16), jax.ShapeDtypeStruct((t, 3 * tn), BF16)],
        scratch_shapes=[pltpu.VMEM((2, tm, LANES), F32), pltpu.VMEM((2, tm, LANES), F32)],
        compiler_params=_cparams("parallel", "arbitrary"),
        name="inproj",
    )(u, w_pack, w_dt, *tables)


SSD_NB = 1


def _ssd_kernel(xbc_ref, z_ref, dt_ref, *refs):
    convw_ref, convb_ref, dtb_ref, alog_ref, dskip_ref, normw_ref = refs[:6]
    o_ref = refs[6]
    xbuf_ref, state_ref, conv_ref, y_ref, acum_ref, rowm_ref = refs[7:]
    for s in range(SSD_NB):
        _ssd_conv(xbc_ref.at[0, s], convw_ref, convb_ref, xbuf_ref.at[s], state_ref.at[s], conv_ref.at[s])
    for s in range(SSD_NB):
        _ssd_scan(z_ref.at[0, s], dt_ref.at[0, s], dtb_ref, alog_ref, dskip_ref, normw_ref, o_ref.at[0, s],
                  state_ref.at[s], conv_ref.at[s], y_ref.at[s], acum_ref.at[s], rowm_ref.at[s])


def _ssd_conv(xbc_ref, convw_ref, convb_ref, xbuf_ref, state_ref, conv_ref):
    q = SSD_CHUNK

    @pl.when(pl.program_id(1) == 0)
    def _():
        xbuf_ref[0:q, :] = jnp.zeros((q, SSD_CONV_CH), BF16)
        state_ref[...] = jnp.zeros_like(state_ref)

    xbuf_ref[q:2 * q, :] = xbc_ref[...]
    ti = lax.broadcasted_iota(jnp.int32, (3 * q, 2 * q), 0)
    si = lax.broadcasted_iota(jnp.int32, (3 * q, 2 * q), 1)
    tap = lax.shift_right_logical(ti, q.bit_length() - 1)
    shift = jnp.where(si == (ti & (q - 1)) + (q - 3) + tap, 1.0, 0.0).astype(BF16)
    cw = 512

    def conv_body(i, c):
        cs = pl.ds(pl.multiple_of(i * cw, cw), cw)
        xx = xbuf_ref[:, cs]
        sh = jnp.dot(shift, xx, preferred_element_type=F32)
        w = convw_ref[:, cs]
        acc = (convb_ref[:, cs] + w[3:4] * xx[q:2 * q].astype(F32) + w[0:1] * sh[0:q] + w[1:2] * sh[q:2 * q]
               + w[2:3] * sh[2 * q:3 * q])
        conv_ref[:, cs] = _silu(acc)
        return c

    lax.fori_loop(0, SSD_CONV_CH // cw, conv_body, 0, unroll=3)
    xbuf_ref[0:q, :] = xbuf_ref[q:2 * q, :]


def _ssd_scan(z_ref, dt_ref, dtb_ref, alog_ref, dskip_ref, normw_ref, o_ref, state_ref, conv_ref, y_ref,
              acum_ref, rowm_ref):
    q = SSD_CHUNK
    hpg = SSD_HEADS // SSD_GROUPS

    dt = _softplus(dt_ref[...] + dtb_ref[...])
    adt = dt * (-jnp.exp(alog_ref[...]))
    rowi = lax.broadcasted_iota(jnp.int32, (q, q), 0)
    coli = lax.broadcasted_iota(jnp.int32, (q, q), 1)
    acum = adt
    for s in (1, 2, 4, 8, 16, 32, 64):
        acum = acum + jnp.where(rowi >= s, pltpu.roll(acum, s, 0), 0.0)
    rowm_ref[...] = acum.T - jnp.log(dt.T)
    for g in range(SSD_GROUPS):
        acum_ref[g] = pltpu.roll(acum, (LANES - hpg * g) % LANES, 1)
    causal = rowi >= coli
    left = coli < SSD_HEAD_DIM

    def group_body(g, c):
        a_g = acum_ref[g]
        rm_g = rowm_ref[pl.ds(pl.multiple_of(g * hpg, hpg), hpg), :]
        b_g = conv_ref[:, pl.ds(pl.multiple_of(SSD_INNER + g * SSD_STATE, LANES), SSD_STATE)]
        c_g = conv_ref[:, pl.ds(pl.multiple_of(SSD_INNER + (SSD_GROUPS + g) * SSD_STATE, LANES), SSD_STATE)]
        c_gb = c_g.astype(BF16)
        cb = lax.dot_general(c_gb, b_g.astype(BF16), (((1,), (1,)), ((), ())), preferred_element_type=F32)
        b_t = b_g.T
        for pp in range(hpg // 2):
            ps = pl.ds(pl.multiple_of((g * (hpg // 2) + pp) * LANES, LANES), LANES)
            m_parts, s_parts, ecols, ealasts = [], [], [], []
            for j in (2 * pp, 2 * pp + 1):
                col = a_g[:, j:j + 1]
                rowm = rm_g[j:j + 1, :]
                alast = a_g[q - 1:q, j:j + 1]
                m_parts.append(cb * jnp.exp(jnp.where(causal, col - rowm, NEG_BIG)))
                s_parts.append(b_t * jnp.exp(alast - rowm))
                ecols.append(jnp.exp(col))
                ealasts.append(jnp.exp(alast))
            lhs = jnp.concatenate([jnp.concatenate(m_parts, axis=1), jnp.concatenate(s_parts, axis=1)],
                                  axis=0).astype(BF16)
            xp = conv_ref[:, ps]
            rhs = jnp.concatenate([jnp.where(left, xp, 0.0), jnp.where(left, 0.0, xp)], axis=0).astype(BF16)
            res = jnp.dot(lhs, rhs, preferred_element_type=F32)
            st_old = state_ref[:, ps]
            y_off = jnp.dot(c_gb, st_old.astype(BF16), preferred_element_type=F32)
            y_ref[:, ps] = res[0:q] + y_off * jnp.where(left, ecols[0], ecols[1]) + dskip_ref[:, ps] * xp
            state_ref[:, ps] = st_old * jnp.where(left, ealasts[0], ealasts[1]) + res[q:2 * q]
        return c

    lax.fori_loop(0, SSD_GROUPS, group_body, 0, unroll=4)

    gw = SSD_INNER // SSD_GROUPS

    def norm_body(g, c):
        gs = pl.ds(pl.multiple_of(g * gw, gw), gw)
        yg = y_ref[:, gs] * _silu(z_ref[:, gs].astype(F32))
        o_ref[:, gs] = (_rms_scale(yg) * normw_ref[:, gs]).astype(o_ref.dtype)
        return c

    lax.fori_loop(0, SSD_GROUPS, norm_body, 0, unroll=4)


def _ssd(pack, dt_raw, conv_w, conv_b, dt_bias, a_log, d_skip, norm_w, batch, seq):
    t = pack.shape[0]
    nc = seq // SSD_CHUNK
    q = SSD_CHUNK
    pad = LANES - SSD_HEADS
    nb = SSD_NB
    const = lambda shape: pl.BlockSpec(shape, lambda b, c: (0, 0))
    rows = lambda width, col: pl.BlockSpec((1, nb, q, width), lambda b, c: (b, 0, c, col))
    pack4 = pack.reshape(batch // nb, nb, seq, PACK_W)
    out = pl.pallas_call(
        _ssd_kernel,
        grid=(batch // nb, nc),
        in_specs=[
            rows(SSD_CONV_CH, 0), rows(SSD_INNER, PACK_Z // 2), rows(LANES, 0),
            const((SSD_CONV, SSD_CONV_CH)), const((1, SSD_CONV_CH)), const((1, LANES)), const((1, LANES)),
            const((1, SSD_INNER)), const((1, SSD_INNER)),
        ],
        out_specs=rows(SSD_INNER, 0),
        out_shape=jax.ShapeDtypeStruct((batch // nb, nb, seq, SSD_INNER), BF16),
        scratch_shapes=[
            pltpu.VMEM((nb, 2 * q, SSD_CONV_CH), BF16),
            pltpu.VMEM((nb, SSD_STATE, SSD_INNER), F32),
            pltpu.VMEM((nb, q, SSD_CONV_CH), F32),
            pltpu.VMEM((nb, q, SSD_INNER), F32),
            pltpu.VMEM((nb, SSD_GROUPS, q, LANES), F32),
            pltpu.VMEM((nb, LANES, q), F32),
        ],
        compiler_params=_cparams("parallel", "arbitrary"),
        name="ssd",
    )(pack4, pack4, dt_raw.reshape(batch // nb, nb, seq, LANES), conv_w, conv_b.reshape(1, -1),
      jnp.pad(dt_bias, (0, pad)).reshape(1, LANES), jnp.pad(a_log, (0, pad)).reshape(1, LANES),
      jnp.repeat(d_skip, SSD_HEAD_DIM).reshape(1, SSD_INNER), norm_w.reshape(1, SSD_INNER))
    return out.reshape(t, SSD_INNER)


ATTN_HB = 4


def _attn_kernel(q_ref, k_ref, v_ref, o_ref, lse_ref, bias_ref, *scratch, dilation):
    assert dilation in (1, 4, 16)
    blk = ATTN_BLOCK
    seq = q_ref.shape[1]
    length = seq // dilation
    nblk = length // blk
    quarter = seq // 4
    direct = dilation == 1
    if not direct:
        nat_ref, tmp_ref, orm_ref, lrm_ref = scratch
    hsl = lambda h: slice(h * LANES, (h + 1) * LANES)

    def merge_rows(src):
        if dilation == 4:
            for r in range(4):
                nat_ref[pl.ds(r, quarter, stride=4), :] = src[r * quarter:(r + 1) * quarter, :]
        else:
            for r in range(4):
                for r2 in range(4):
                    c = r * 4 + r2
                    tmp_ref[pl.ds(r * quarter + r2, length, stride=4), :] = src[c * length:(c + 1) * length, :]
            for r in range(4):
                nat_ref[pl.ds(r, quarter, stride=4), :] = tmp_ref[r * quarter:(r + 1) * quarter, :]

    qi = lax.broadcasted_iota(jnp.int32, (blk, 2 * blk), 0)
    kj = lax.broadcasted_iota(jnp.int32, (blk, 2 * blk), 1)
    lane = lax.broadcasted_iota(jnp.int32, (blk, LANES), 1)
    bias_ref[0] = jnp.where(kj <= qi, 0.0, NEG_BIG)
    bias_ref[1] = jnp.where((kj >= qi) & (kj <= qi + blk), 0.0, NEG_BIG)
    bias_ref[2] = jnp.where((kj >= blk) & (kj <= qi + blk), 0.0, NEG_BIG)

    def block(i, c):
        r0 = pl.multiple_of(i * blk, blk)
        k0 = pl.multiple_of(jnp.maximum(r0 - blk, 0), blk)
        bias = bias_ref[jnp.where(i == 0, 0, jnp.where(i % nblk == 0, 2, 1))]
        lse_tile = jnp.zeros((blk, LANES), F32)
        for h in range(ATTN_HB):
            s = lax.dot_general(q_ref[0, pl.ds(r0, blk), hsl(h)], k_ref[0, pl.ds(k0, 2 * blk), hsl(h)], NT_DIMS,
                                preferred_element_type=F32)
            s = s + bias
            m = jnp.max(s, axis=-1, keepdims=True)
            p = jnp.exp2(s - m).astype(BF16)
            vx = jnp.concatenate([v_ref[0, pl.ds(k0, 2 * blk), hsl(h)], jnp.ones((2 * blk, LANES), BF16)], axis=1)
            r = jnp.dot(p, vx, preferred_element_type=F32)
            l = r[:, LANES:]
            o = r[:, :LANES] * (1.0 / l)
            if direct:
                o_ref[0, pl.ds(r0, blk), hsl(h)] = o.astype(o_ref.dtype)
            else:
                orm_ref[h, pl.ds(r0, blk), :] = o
            lse_tile = jnp.where(lane == h, m + jnp.log(l) * (1.0 / LN2), lse_tile)
        if direct:
            lse_ref[0, pl.ds(r0, blk), :] = lse_tile
        else:
            lrm_ref[pl.ds(r0, blk), :] = lse_tile
        return c

    lax.fori_loop(0, seq // blk, block, 0, unroll=4)

    if not direct:
        for h in range(ATTN_HB):
            merge_rows(orm_ref.at[h])
            o_ref[0, :, hsl(h)] = nat_ref[...].astype(o_ref.dtype)
        merge_rows(lrm_ref)
        lse_ref[0] = nat_ref[...]


def _attention_group(src3, units, dilation):
    batch, seq, _ = src3.shape
    halves = ATTN_HEADS // ATTN_HB
    wb = ATTN_HB * ATTN_HEAD_DIM
    col = lambda unit: (lambda b, hh: (b, 0, unit * halves + hh))
    scratch = [] if dilation == 1 else [
        pltpu.VMEM((seq, LANES), F32),
        pltpu.VMEM((seq, LANES), F32),
        pltpu.VMEM((ATTN_HB, seq, LANES), F32),
        pltpu.VMEM((seq, LANES), F32),
    ]
    out, lse = pl.pallas_call(
        functools.partial(_attn_kernel, dilation=dilation),
        grid=(batch, halves),
        in_specs=[pl.BlockSpec((1, seq, wb), col(u)) for u in units],
        out_specs=[
            pl.BlockSpec((1, seq, wb), lambda b, hh: (b, 0, hh)),
            pl.BlockSpec((1, seq, LANES), lambda b, hh: (b, 0, hh)),
        ],
        out_shape=[
            jax.ShapeDtypeStruct((batch, seq, ATTN_OUT), BF16),
            jax.ShapeDtypeStruct((batch, seq, halves * LANES), F32),
        ],
        scratch_shapes=[pltpu.VMEM((3, ATTN_BLOCK, 2 * ATTN_BLOCK), F32)] + scratch,
        compiler_params=_cparams("parallel", "parallel"),
        name=f"attn_d{dilation}",
    )(src3, src3, src3)
    return out.reshape(batch * seq, ATTN_OUT), lse.reshape(batch * seq, halves * LANES)


def _rope_tables(seq):
    half = ROPE_DIM // 2
    inv = ROPE_THETA ** (-jnp.arange(0, ROPE_DIM, 2, dtype=F32) / ROPE_DIM)
    ang = jnp.arange(seq, dtype=F32)[:, None] * inv[None, :]
    c, s = jnp.cos(ang), jnp.sin(ang)
    gap = LANES // 2 - half
    one, zero = jnp.ones((seq, gap), F32), jnp.zeros((seq, gap), F32)
    cos = jnp.concatenate([c, one, c, one], axis=1)
    sin = jnp.concatenate([-s, zero, s, zero], axis=1)
    return cos, sin


def _merge_kernel(yssd_ref, o0_ref, o1_ref, o2_ref, l0_ref, l1_ref, l2_ref, gs_ref, ga_ref, h_ref,
                  wa_ref, wb_ref, wo_ref, nw_ref, hout_ref, u_ref, yattn_ref):
    l0, l1, l2 = l0_ref[...], l1_ref[...], l2_ref[...]
    m = jnp.maximum(jnp.maximum(l0, l1), l2)
    e0, e1, e2 = jnp.exp2(l0 - m), jnp.exp2(l1 - m), jnp.exp2(l2 - m)
    inv = 1.0 / (e0 + e1 + e2)
    w0, w1, w2 = e0 * inv, e1 * inv, e2 * inv
    for h in range(ATTN_HEADS):
        sl = slice(h * LANES, (h + 1) * LANES)
        c = (h // ATTN_HB) * LANES + h % ATTN_HB
        yh = (w0[:, c:c + 1] * o0_ref[:, sl].astype(F32) + w1[:, c:c + 1] * o1_ref[:, sl].astype(F32)
              + w2[:, c:c + 1] * o2_ref[:, sl].astype(F32))
        yattn_ref[:, sl] = yh.astype(BF16)
    a = jnp.dot(yssd_ref[...], wa_ref[...], preferred_element_type=F32)
    b = jnp.dot(yattn_ref[...], wb_ref[...], preferred_element_type=F32)
    merged = _sigmoid(gs_ref[...].astype(F32)) * a + _sigmoid(ga_ref[...].astype(F32)) * b
    hn = h_ref[...] + jnp.dot(merged.astype(BF16), wo_ref[...], preferred_element_type=F32)
    hout_ref[...] = hn
    u_ref[...] = (_rms_scale(hn) * nw_ref[...]).astype(u_ref.dtype)


def _resident(a):
    return pl.BlockSpec(a.shape, lambda *_: (0,) * a.ndim, pipeline_mode=pl.Buffered(1))


def _merge(y_ssd, outs, lses, pack, h, w_a, w_b, w_o, norm_w, tm=512):
    t = h.shape[0]
    d = D_MODEL
    rows = lambda w, col=0: pl.BlockSpec((tm, w), lambda i, col=col: (i, col))
    nw = norm_w.reshape(1, d)
    lw = lses[0].shape[1]
    return pl.pallas_call(
        _merge_kernel,
        grid=(t // tm,),
        in_specs=[rows(SSD_INNER), rows(d), rows(d), rows(d), rows(lw), rows(lw), rows(lw),
                  rows(d, PACK_GS), rows(d, PACK_GA), rows(d), _resident(w_a), _resident(w_b), _resident(w_o),
                  _resident(nw)],
        out_specs=[rows(d), rows(d)],
        out_shape=[jax.ShapeDtypeStruct((t, d), F32), jax.ShapeDtypeStruct((t, d), BF16)],
        scratch_shapes=[pltpu.VMEM((tm, ATTN_OUT), BF16)],
        compiler_params=_cparams("parallel"),
        name="merge",
    )(y_ssd, *outs, *lses, pack, pack, h, w_a, w_b, w_o, nw)


FFN_TH = 256


def _ffn_kernel(u_ref, wgu_ref, wd_ref, h_ref, nw_ref, *rest, last):
    if last:
        out_ref, acc_ref = rest
    else:
        hout_ref, u_out_ref, acc_ref = rest
    acc_ref[...] = h_ref[...]
    u = u_ref[...]

    def hidden_step(j, c):
        lo = pl.multiple_of(j * FFN_TH, FFN_TH)
        gate = jnp.dot(u, wgu_ref[:, pl.ds(lo, FFN_TH)], preferred_element_type=F32)
        up = jnp.dot(u, wgu_ref[:, pl.ds(lo + FFN_HIDDEN, FFN_TH)], preferred_element_type=F32)
        act = (_silu(gate) * up).astype(BF16)
        acc_ref[...] += jnp.dot(act, wd_ref[pl.ds(lo, FFN_TH), :], preferred_element_type=F32)
        return c

    lax.fori_loop(0, FFN_HIDDEN // FFN_TH, hidden_step, 0)
    hn = acc_ref[...]
    normed = _rms_scale(hn) * nw_ref[...]
    if last:
        out_ref[...] = normed
    else:
        hout_ref[...] = hn
        u_out_ref[...] = normed.astype(u_out_ref.dtype)


def _ffn(u, w_gu, w_d, h, norm_w, last, tm=1024):
    t, d = u.shape
    rows = lambda: pl.BlockSpec((tm, d), lambda i: (i, 0))
    nw = norm_w.reshape(1, d)
    if last:
        out_specs = rows()
        out_shape = jax.ShapeDtypeStruct((t, d), F32)
    else:
        out_specs = [rows(), rows()]
        out_shape = [jax.ShapeDtypeStruct((t, d), F32), jax.ShapeDtypeStruct((t, d), BF16)]
    return pl.pallas_call(
        functools.partial(_ffn_kernel, last=last),
        grid=(t // tm,),
        in_specs=[rows(), _resident(w_gu), _resident(w_d), rows(), _resident(nw)],
        out_specs=out_specs,
        out_shape=out_shape,
        scratch_shapes=[pltpu.VMEM((tm, d), F32)],
        compiler_params=_cparams("parallel"),
        name="ffn_last" if last else "ffn",
    )(u, w_gu, w_d, h, nw)


def kernel(x, norm_mix, w_in, conv_w, conv_b, dt_bias, a_log, d_skip, ssd_norm, w_ssd_branch, w_attn_branch,
           w_out, norm_ffn, w_gate_up, w_down, norm_final):
    batch, seq, d = x.shape
    depth = w_in.shape[0]
    t = batch * seq
    tables = _rope_tables(seq)
    h = x.reshape(t, d)
    u = _rmsnorm(h, norm_mix[0])
    for layer in range(depth):
        w_pack, w_dt = _pack_in_weights(w_in, layer)
        pack, dt_raw, by4, by16 = _inproj(u, w_pack, w_dt, tables, seq)
        y_ssd = _ssd(pack, dt_raw, conv_w[layer], conv_b[layer], dt_bias[layer], a_log[layer], d_skip[layer],
                     ssd_norm[layer], batch, seq)
        assert [w // d_ for w, d_ in ATTN_PATTERNS] == [ATTN_BLOCK] * 3 and [d_ for _, d_ in ATTN_PATTERNS] == [1, 4, 16]
        outs, lses = zip(_attention_group(pack.reshape(batch, seq, PACK_W), (PACK_Q, PACK_K, PACK_V), 1),
                         _attention_group(by4.reshape(batch, seq, -1), (0, 1, 2), 4),
                         _attention_group(by16.reshape(batch, seq, -1), (0, 1, 2), 16))
        h, u_ffn = _merge(y_ssd, outs, lses, pack, h, w_ssd_branch[layer].astype(BF16),
                          w_attn_branch[layer].astype(BF16), w_out[layer].astype(BF16), norm_ffn[layer])
        last = layer == depth - 1
        nxt = norm_final if last else norm_mix[layer + 1]
        res = _ffn(u_ffn, w_gate_up[layer].astype(BF16), w_down[layer].astype(BF16), h, nxt, last)
        if last:
            return res.reshape(batch, seq, d)
        h, u = res
```

```python
import functools
import math

import jax
import jax.numpy as jnp
from jax import lax
from jax.experimental import pallas as pl
from jax.experimental.pallas import tpu as pltpu

F32 = jnp.float32
BF16 = jnp.bfloat16

D_MODEL = 1024
RMS_EPS = 1e-5

SSD_INNER = 2048
SSD_HEAD_DIM = 64
SSD_HEADS = 32
SSD_STATE = 128
SSD_GROUPS = 4
SSD_CONV = 4
SSD_CHUNK = 128
SSD_BC = 2 * SSD_GROUPS * SSD_STATE
SSD_CONV_CH = SSD_INNER + SSD_BC

ATTN_HEAD_DIM = 128
ATTN_HEADS = 8
ATTN_PATTERNS = ((128, 1), (512, 4), (2048, 16))
ATTN_BLOCK = 128
ROPE_THETA = 500000.0
ROPE_DIM = 32
ATTN_OUT = ATTN_HEADS * ATTN_HEAD_DIM

FFN_HIDDEN = 2816

PACK_W = 12 * 1024
PACK_Q, PACK_Z, PACK_K, PACK_V, PACK_GS, PACK_GA = 3, 6, 8, 9, 10, 11

Q_SCALE = ATTN_HEAD_DIM ** -0.5 * math.log2(math.e)
LN2 = math.log(2.0)

LANES = 128
NEG_BIG = -1e30
VMEM_LIMIT = 56 * 1024 * 1024


def _cparams(*sem):
    return pltpu.CompilerParams(dimension_semantics=sem, vmem_limit_bytes=VMEM_LIMIT)


def _sigmoid(x):
    return 1.0 / (1.0 + jnp.exp(-x))


def _silu(x):
    h = 0.5 * x
    return h + h * jnp.tanh(h)


def _softplus(x):
    return jnp.maximum(x, 0.0) + jnp.log(1.0 + jnp.exp(-jnp.abs(x)))


def _rms_scale(x):
    return x * lax.rsqrt(jnp.mean(x * x, axis=-1, keepdims=True) + RMS_EPS)


def _rmsnorm_kernel(x_ref, w_ref, o_ref):
    o_ref[...] = (_rms_scale(x_ref[...]) * w_ref[...]).astype(o_ref.dtype)


def _rmsnorm(x, w, tm=1024):
    t, d = x.shape
    return pl.pallas_call(
        _rmsnorm_kernel,
        grid=(t // tm,),
        in_specs=[pl.BlockSpec((tm, d), lambda i: (i, 0)), pl.BlockSpec((1, d), lambda i: (0, 0))],
        out_specs=pl.BlockSpec((tm, d), lambda i: (i, 0)),
        out_shape=jax.ShapeDtypeStruct((t, d), BF16),
        compiler_params=_cparams("parallel"),
        name="rmsnorm",
    )(x, w.reshape(1, d))


IN_DT_COL = SSD_INNER + SSD_CONV_CH
NT_DIMS = (((1,), (1,)), ((), ()))


def _pack_kernel(a_ref, o_ref):
    j = pl.program_id(0)
    rotary = (j >= PACK_Q) & (j < PACK_Z) | (j == PACK_K)
    half, mid = ROPE_DIM // 2, ATTN_HEAD_DIM // 2

    @pl.when(jnp.logical_not(rotary))
    def _():
        o_ref[...] = a_ref[0].astype(o_ref.dtype)

    @pl.when(rotary)
    def _():
        for h in range(a_ref.shape[1] // ATTN_HEAD_DIM):
            b = h * ATTN_HEAD_DIM
            for dst, src, n in ((0, 0, half), (half, mid, half), (2 * half, 2 * half, mid - 2 * half),
                                (mid, half, half), (mid + half, mid + half, mid - half)):
                o_ref[b + dst:b + dst + n, :] = a_ref[0, b + src:b + src + n, :].astype(o_ref.dtype)


def _pack_in_weights(w_in, layer):
    _, d, _ = w_in.shape
    unit = ATTN_OUT
    w_t = jnp.swapaxes(w_in, 1, 2)
    late = IN_DT_COL + SSD_HEADS
    g = SSD_HEADS
    ug, lg = unit // g, late // g
    src = lambda j: g * jnp.where(j < PACK_Q, SSD_INNER // g + j * ug,
                                  jnp.where(j < PACK_Z, lg + (j - PACK_Q) * ug,
                                            jnp.where(j < PACK_K, (j - PACK_Z) * ug, lg + (j - PACK_Q - 2) * ug)))
    packed = pl.pallas_call(
        _pack_kernel,
        grid=(PACK_W // unit,),
        in_specs=[pl.BlockSpec((pl.Element(1), pl.Element(unit), pl.Element(d)), lambda j: (layer, src(j), 0))],
        out_specs=pl.BlockSpec((unit, d), lambda j: (j, 0)),
        out_shape=jax.ShapeDtypeStruct((PACK_W, d), BF16),
        compiler_params=_cparams("arbitrary"),
        name="pack_w_in",
    )(w_t)
    w_dt = jnp.pad(w_t[layer, IN_DT_COL:late, :], ((0, LANES - SSD_HEADS), (0, 0)))
    return packed, w_dt


def _inproj_kernel(u_ref, w_ref, wdt_ref, cos_ref, sin_ref, o_ref, dt_ref):
    j = pl.program_id(1)
    is_q = (j >= PACK_Q) & (j < PACK_Q + len(ATTN_PATTERNS))
    rotary = is_q | (j == PACK_K)

    @pl.when(rotary)
    def _():
        res = lax.dot_general(u_ref[...], w_ref[...], NT_DIMS, preferred_element_type=F32)
        scale = jnp.where(is_q, Q_SCALE, 1.0)
        cos, sin = cos_ref[...] * scale, sin_ref[...] * scale
        for h in range(res.shape[1] // LANES):
            x = res[:, h * LANES:(h + 1) * LANES]
            o_ref[:, h * LANES:(h + 1) * LANES] = (x * cos + pltpu.roll(x, LANES // 2, 1) * sin).astype(o_ref.dtype)

    @pl.when(jnp.logical_not(rotary))
    def _():
        o_ref[...] = lax.dot_general(u_ref[...], w_ref[...], NT_DIMS, preferred_element_type=F32).astype(o_ref.dtype)

    @pl.when(j == 0)
    def _():
        dt_ref[...] = lax.dot_general(u_ref[...], wdt_ref[...].astype(BF16), NT_DIMS, preferred_element_type=F32)


def _inproj(u, w_pack, w_dt, tables, tm=2048):
    t, d = u.shape
    n = w_pack.shape[0]
    tn = ATTN_OUT
    seq = tables[0].shape[0]
    tab_spec = pl.BlockSpec((tm, LANES), lambda i, j: (i % (seq // tm), 0))
    return pl.pallas_call(
        _inproj_kernel,
        grid=(t // tm, n // tn),
        in_specs=[
            pl.BlockSpec((tm, d), lambda i, j: (i, 0)),
            pl.BlockSpec((tn, d), lambda i, j: (j, 0)),
            pl.BlockSpec((LANES, d), lambda i, j: (0, 0)),
            tab_spec, tab_spec,
        ],
        out_specs=[
            pl.BlockSpec((tm, tn), lambda i, j: (i, j)),
            pl.BlockSpec((tm, LANES), lambda i, j: (i, 0)),
        ],
        out_shape=[jax.ShapeDtypeStruct((t, n), BF16), jax.ShapeDtypeStruct((t, LANES), F32)],
        compiler_params=_cparams("parallel", "arbitrary"),
        name="inproj",
    )(u, w_pack, w_dt, *tables)


SSD_NB = 1


def _ssd_kernel(xbc_ref, z_ref, dt_ref, *refs):
    convw_ref, convb_ref, dtb_ref, alog_ref, dskip_ref, normw_ref = refs[:6]
    o_ref = refs[6]
    xbuf_ref, state_ref, conv_ref, y_ref, acum_ref, rowm_ref = refs[7:]
    for s in range(SSD_NB):
        _ssd_conv(xbc_ref.at[0, s], convw_ref, convb_ref, xbuf_ref.at[s], state_ref.at[s], conv_ref.at[s])
    for s in range(SSD_NB):
        _ssd_scan(z_ref.at[0, s], dt_ref.at[0, s], dtb_ref, alog_ref, dskip_ref, normw_ref, o_ref.at[0, s],
                  state_ref.at[s], conv_ref.at[s], y_ref.at[s], acum_ref.at[s], rowm_ref.at[s])


def _ssd_conv(xbc_ref, convw_ref, convb_ref, xbuf_ref, state_ref, conv_ref):
    q = SSD_CHUNK

    @pl.when(pl.program_id(1) == 0)
    def _():
        xbuf_ref[0:q, :] = jnp.zeros((q, SSD_CONV_CH), BF16)
        state_ref[...] = jnp.zeros_like(state_ref)

    xbuf_ref[q:2 * q, :] = xbc_ref[...]
    ti = lax.broadcasted_iota(jnp.int32, (3 * q, 2 * q), 0)
    si = lax.broadcasted_iota(jnp.int32, (3 * q, 2 * q), 1)
    tap = lax.shift_right_logical(ti, q.bit_length() - 1)
    shift = jnp.where(si == (ti & (q - 1)) + (q - 3) + tap, 1.0, 0.0).astype(BF16)
    cw = 512

    def conv_body(i, c):
        cs = pl.ds(pl.multiple_of(i * cw, cw), cw)
        xx = xbuf_ref[:, cs]
        sh = jnp.dot(shift, xx, preferred_element_type=F32)
        w = convw_ref[:, cs]
        acc = (convb_ref[:, cs] + w[3:4] * xx[q:2 * q].astype(F32) + w[0:1] * sh[0:q] + w[1:2] * sh[q:2 * q]
               + w[2:3] * sh[2 * q:3 * q])
        conv_ref[:, cs] = _silu(acc)
        return c

    lax.fori_loop(0, SSD_CONV_CH // cw, conv_body, 0, unroll=3)
    xbuf_ref[0:q, :] = xbuf_ref[q:2 * q, :]


def _ssd_scan(z_ref, dt_ref, dtb_ref, alog_ref, dskip_ref, normw_ref, o_ref, state_ref, conv_ref, y_ref,
              acum_ref, rowm_ref):
    q = SSD_CHUNK
    hpg = SSD_HEADS // SSD_GROUPS

    dt = _softplus(dt_ref[...] + dtb_ref[...])
    adt = dt * (-jnp.exp(alog_ref[...]))
    rowi = lax.broadcasted_iota(jnp.int32, (q, q), 0)
    coli = lax.broadcasted_iota(jnp.int32, (q, q), 1)
    acum = adt
    for s in (1, 2, 4, 8, 16, 32, 64):
        acum = acum + jnp.where(rowi >= s, pltpu.roll(acum, s, 0), 0.0)
    rowm_ref[...] = acum.T - jnp.log(dt.T)
    for g in range(SSD_GROUPS):
        acum_ref[g] = pltpu.roll(acum, (LANES - hpg * g) % LANES, 1)
    causal = rowi >= coli
    left = coli < SSD_HEAD_DIM

    def group_body(g, c):
        a_g = acum_ref[g]
        rm_g = rowm_ref[pl.ds(pl.multiple_of(g * hpg, hpg), hpg), :]
        b_g = conv_ref[:, pl.ds(pl.multiple_of(SSD_INNER + g * SSD_STATE, LANES), SSD_STATE)]
        c_g = conv_ref[:, pl.ds(pl.multiple_of(SSD_INNER + (SSD_GROUPS + g) * SSD_STATE, LANES), SSD_STATE)]
        c_gb = c_g.astype(BF16)
        cb = lax.dot_general(c_gb, b_g.astype(BF16), (((1,), (1,)), ((), ())), preferred_element_type=F32)
        b_t = b_g.T
        for pp in range(hpg // 2):
            ps = pl.ds(pl.multiple_of((g * (hpg // 2) + pp) * LANES, LANES), LANES)
            m_parts, s_parts, ecols, ealasts = [], [], [], []
            for j in (2 * pp, 2 * pp + 1):
                col = a_g[:, j:j + 1]
                rowm = rm_g[j:j + 1, :]
                alast = a_g[q - 1:q, j:j + 1]
                m_parts.append(cb * jnp.exp(jnp.where(causal, col - rowm, NEG_BIG)))
                s_parts.append(b_t * jnp.exp(alast - rowm))
                ecols.append(jnp.exp(col))
                ealasts.append(jnp.exp(alast))
            lhs = jnp.concatenate([jnp.concatenate(m_parts, axis=1), jnp.concatenate(s_parts, axis=1)],
                                  axis=0).astype(BF16)
            xp = conv_ref[:, ps]
            rhs = jnp.concatenate([jnp.where(left, xp, 0.0), jnp.where(left, 0.0, xp)], axis=0).astype(BF16)
            res = jnp.dot(lhs, rhs, preferred_element_type=F32)
            st_old = state_ref[:, ps]
            y_off = jnp.dot(c_gb, st_old.astype(BF16), preferred_element_type=F32)
            y_ref[:, ps] = res[0:q] + y_off * jnp.where(left, ecols[0], ecols[1]) + dskip_ref[:, ps] * xp
            state_ref[:, ps] = st_old * jnp.where(left, ealasts[0], ealasts[1]) + res[q:2 * q]
        return c

    lax.fori_loop(0, SSD_GROUPS, group_body, 0, unroll=4)

    gw = SSD_INNER // SSD_GROUPS

    def norm_body(g, c):
        gs = pl.ds(pl.multiple_of(g * gw, gw), gw)
        yg = y_ref[:, gs] * _silu(z_ref[:, gs].astype(F32))
        o_ref[:, gs] = (_rms_scale(yg) * normw_ref[:, gs]).astype(o_ref.dtype)
        return c

    lax.fori_loop(0, SSD_GROUPS, norm_body, 0, unroll=4)


def _ssd(pack, dt_raw, conv_w, conv_b, dt_bias, a_log, d_skip, norm_w, batch, seq):
    t = pack.shape[0]
    nc = seq // SSD_CHUNK
    q = SSD_CHUNK
    pad = LANES - SSD_HEADS
    nb = SSD_NB
    const = lambda shape: pl.BlockSpec(shape, lambda b, c: (0, 0))
    rows = lambda width, col: pl.BlockSpec((1, nb, q, width), lambda b, c: (b, 0, c, col))
    pack4 = pack.reshape(batch // nb, nb, seq, PACK_W)
    out = pl.pallas_call(
        _ssd_kernel,
        grid=(batch // nb, nc),
        in_specs=[
            rows(SSD_CONV_CH, 0), rows(SSD_INNER, PACK_Z // 2), rows(LANES, 0),
            const((SSD_CONV, SSD_CONV_CH)), const((1, SSD_CONV_CH)), const((1, LANES)), const((1, LANES)),
            const((1, SSD_INNER)), const((1, SSD_INNER)),
        ],
        out_specs=rows(SSD_INNER, 0),
        out_shape=jax.ShapeDtypeStruct((batch // nb, nb, seq, SSD_INNER), BF16),
        scratch_shapes=[
            pltpu.VMEM((nb, 2 * q, SSD_CONV_CH), BF16),
            pltpu.VMEM((nb, SSD_STATE, SSD_INNER), F32),
            pltpu.VMEM((nb, q, SSD_CONV_CH), F32),
            pltpu.VMEM((nb, q, SSD_INNER), F32),
            pltpu.VMEM((nb, SSD_GROUPS, q, LANES), F32),
            pltpu.VMEM((nb, LANES, q), F32),
        ],
        compiler_params=_cparams("parallel", "arbitrary"),
        name="ssd",
    )(pack4, pack4, dt_raw.reshape(batch // nb, nb, seq, LANES), conv_w, conv_b.reshape(1, -1),
      jnp.pad(dt_bias, (0, pad)).reshape(1, LANES), jnp.pad(a_log, (0, pad)).reshape(1, LANES),
      jnp.repeat(d_skip, SSD_HEAD_DIM).reshape(1, SSD_INNER), norm_w.reshape(1, SSD_INNER))
    return out.reshape(t, SSD_INNER)


ATTN_HB = 4


def _attn_kernel(q_ref, k_ref, v_ref, o_ref, lse_ref, bias_ref, *scratch, dilation):
    assert dilation in (1, 4, 16)
    blk = ATTN_BLOCK
    seq = q_ref.shape[1]
    length = seq // dilation
    nblk = length // blk
    quarter = seq // 4
    direct = dilation == 1
    if not direct:
        nat_ref, tmp_ref, qrm_ref, krm_ref, vrm_ref, orm_ref, lrm_ref = scratch
    hsl = lambda h: slice(h * LANES, (h + 1) * LANES)

    def split_rows(src_ref, dst_ref, h):
        nat_ref[...] = src_ref[0, :, hsl(h)].astype(F32)
        if dilation == 4:
            for r in range(4):
                dst_ref[r * quarter:(r + 1) * quarter, hsl(h)] = (
                    nat_ref[pl.ds(r, quarter, stride=4), :].astype(BF16))
        else:
            for r in range(4):
                tmp_ref[r * quarter:(r + 1) * quarter, :] = nat_ref[pl.ds(r, quarter, stride=4), :]
            for r in range(4):
                for r2 in range(4):
                    c = r * 4 + r2
                    dst_ref[c * length:(c + 1) * length, hsl(h)] = (
                        tmp_ref[pl.ds(r * quarter + r2, length, stride=4), :].astype(BF16))

    def merge_rows(src):
        if dilation == 4:
            for r in range(4):
                nat_ref[pl.ds(r, quarter, stride=4), :] = src[r * quarter:(r + 1) * quarter, :]
        else:
            for r in range(4):
                for r2 in range(4):
                    c = r * 4 + r2
                    tmp_ref[pl.ds(r * quarter + r2, length, stride=4), :] = src[c * length:(c + 1) * length, :]
            for r in range(4):
                nat_ref[pl.ds(r, quarter, stride=4), :] = tmp_ref[r * quarter:(r + 1) * quarter, :]

    if direct:
        rows = lambda ref, h, r0, n: ref[0, pl.ds(r0, n), hsl(h)]
        qs, ks, vs = q_ref, k_ref, v_ref
    else:
        for h in range(ATTN_HB):
            split_rows(q_ref, qrm_ref, h)
            split_rows(k_ref, krm_ref, h)
            split_rows(v_ref, vrm_ref, h)
        rows = lambda ref, h, r0, n: ref[pl.ds(r0, n), hsl(h)]
        qs, ks, vs = qrm_ref, krm_ref, vrm_ref

    qi = lax.broadcasted_iota(jnp.int32, (blk, 2 * blk), 0)
    kj = lax.broadcasted_iota(jnp.int32, (blk, 2 * blk), 1)
    lane = lax.broadcasted_iota(jnp.int32, (blk, LANES), 1)
    bias_ref[0] = jnp.where(kj <= qi, 0.0, NEG_BIG)
    bias_ref[1] = jnp.where((kj >= qi) & (kj <= qi + blk), 0.0, NEG_BIG)
    bias_ref[2] = jnp.where((kj >= blk) & (kj <= qi + blk), 0.0, NEG_BIG)

    def block(i, c):
        r0 = pl.multiple_of(i * blk, blk)
        k0 = pl.multiple_of(jnp.maximum(r0 - blk, 0), blk)
        bias = bias_ref[jnp.where(i == 0, 0, jnp.where(i % nblk == 0, 2, 1))]
        lse_tile = jnp.zeros((blk, LANES), F32)
        for h in range(ATTN_HB):
            s = lax.dot_general(rows(qs, h, r0, blk), rows(ks, h, k0, 2 * blk), NT_DIMS, preferred_element_type=F32)
            s = s + bias
            m = jnp.max(s, axis=-1, keepdims=True)
            p = jnp.exp2(s - m).astype(BF16)
            vx = jnp.concatenate([rows(vs, h, k0, 2 * blk), jnp.ones((2 * blk, LANES), BF16)], axis=1)
            r = jnp.dot(p, vx, preferred_element_type=F32)
            l = r[:, LANES:]
            o = r[:, :LANES] * (1.0 / l)
            if direct:
                o_ref[0, pl.ds(r0, blk), hsl(h)] = o.astype(o_ref.dtype)
            else:
                orm_ref[h, pl.ds(r0, blk), :] = o
            lse_tile = jnp.where(lane == h, m + jnp.log(l) * (1.0 / LN2), lse_tile)
        if direct:
            lse_ref[0, pl.ds(r0, blk), :] = lse_tile
        else:
            lrm_ref[pl.ds(r0, blk), :] = lse_tile
        return c

    lax.fori_loop(0, seq // blk, block, 0, unroll=8)

    if not direct:
        for h in range(ATTN_HB):
            merge_rows(orm_ref.at[h])
            o_ref[0, :, hsl(h)] = nat_ref[...].astype(o_ref.dtype)
        merge_rows(lrm_ref)
        lse_ref[0] = nat_ref[...]


def _attention_group(src3, units, dilation):
    batch, seq, _ = src3.shape
    halves = ATTN_HEADS // ATTN_HB
    wb = ATTN_HB * ATTN_HEAD_DIM
    col = lambda unit: (lambda b, hh: (b, 0, unit * halves + hh))
    scratch = [] if dilation == 1 else [
        pltpu.VMEM((seq, LANES), F32),
        pltpu.VMEM((seq, LANES), F32),
        pltpu.VMEM((seq, wb), BF16),
        pltpu.VMEM((seq, wb), BF16),
        pltpu.VMEM((seq, wb), BF16),
        pltpu.VMEM((ATTN_HB, seq, LANES), F32),
        pltpu.VMEM((seq, LANES), F32),
    ]
    out, lse = pl.pallas_call(
        functools.partial(_attn_kernel, dilation=dilation),
        grid=(batch, halves),
        in_specs=[pl.BlockSpec((1, seq, wb), col(u)) for u in units],
        out_specs=[
            pl.BlockSpec((1, seq, wb), lambda b, hh: (b, 0, hh)),
            pl.BlockSpec((1, seq, LANES), lambda b, hh: (b, 0, hh)),
        ],
        out_shape=[
            jax.ShapeDtypeStruct((batch, seq, ATTN_OUT), BF16),
            jax.ShapeDtypeStruct((batch, seq, halves * LANES), F32),
        ],
        scratch_shapes=[pltpu.VMEM((3, ATTN_BLOCK, 2 * ATTN_BLOCK), F32)] + scratch,
        compiler_params=_cparams("parallel", "parallel"),
        name=f"attn_d{dilation}",
    )(src3, src3, src3)
    return out.reshape(batch * seq, ATTN_OUT), lse.reshape(batch * seq, halves * LANES)


def _rope_tables(seq):
    half = ROPE_DIM // 2
    inv = ROPE_THETA ** (-jnp.arange(0, ROPE_DIM, 2, dtype=F32) / ROPE_DIM)
    ang = jnp.arange(seq, dtype=F32)[:, None] * inv[None, :]
    c, s = jnp.cos(ang), jnp.sin(ang)
    gap = LANES // 2 - half
    one, zero = jnp.ones((seq, gap), F32), jnp.zeros((seq, gap), F32)
    cos = jnp.concatenate([c, one, c, one], axis=1)
    sin = jnp.concatenate([-s, zero, s, zero], axis=1)
    return cos, sin


def _merge_kernel(yssd_ref, o0_ref, o1_ref, o2_ref, l0_ref, l1_ref, l2_ref, gs_ref, ga_ref, h_ref,
                  wa_ref, wb_ref, wo_ref, nw_ref, hout_ref, u_ref, yattn_ref):
    l0, l1, l2 = l0_ref[...], l1_ref[...], l2_ref[...]
    m = jnp.maximum(jnp.maximum(l0, l1), l2)
    e0, e1, e2 = jnp.exp2(l0 - m), jnp.exp2(l1 - m), jnp.exp2(l2 - m)
    inv = 1.0 / (e0 + e1 + e2)
    w0, w1, w2 = e0 * inv, e1 * inv, e2 * inv
    for h in range(ATTN_HEADS):
        sl = slice(h * LANES, (h + 1) * LANES)
        c = (h // ATTN_HB) * LANES + h % ATTN_HB
        yh = (w0[:, c:c + 1] * o0_ref[:, sl].astype(F32) + w1[:, c:c + 1] * o1_ref[:, sl].astype(F32)
              + w2[:, c:c + 1] * o2_ref[:, sl].astype(F32))
        yattn_ref[:, sl] = yh.astype(BF16)
    a = jnp.dot(yssd_ref[...], wa_ref[...], preferred_element_type=F32)
    b = jnp.dot(yattn_ref[...], wb_ref[...], preferred_element_type=F32)
    merged = _sigmoid(gs_ref[...].astype(F32)) * a + _sigmoid(ga_ref[...].astype(F32)) * b
    hn = h_ref[...] + jnp.dot(merged.astype(BF16), wo_ref[...], preferred_element_type=F32)
    hout_ref[...] = hn
    u_ref[...] = (_rms_scale(hn) * nw_ref[...]).astype(u_ref.dtype)


def _resident(a):
    return pl.BlockSpec(a.shape, lambda *_: (0,) * a.ndim, pipeline_mode=pl.Buffered(1))


def _merge(y_ssd, outs, lses, pack, h, w_a, w_b, w_o, norm_w, tm=512):
    t = h.shape[0]
    d = D_MODEL
    rows = lambda w, col=0: pl.BlockSpec((tm, w), lambda i, col=col: (i, col))
    nw = norm_w.reshape(1, d)
    lw = lses[0].shape[1]
    return pl.pallas_call(
        _merge_kernel,
        grid=(t // tm,),
        in_specs=[rows(SSD_INNER), rows(d), rows(d), rows(d), rows(lw), rows(lw), rows(lw),
                  rows(d, PACK_GS), rows(d, PACK_GA), rows(d), _resident(w_a), _resident(w_b), _resident(w_o),
                  _resident(nw)],
        out_specs=[rows(d), rows(d)],
        out_shape=[jax.ShapeDtypeStruct((t, d), F32), jax.ShapeDtypeStruct((t, d), BF16)],
        scratch_shapes=[pltpu.VMEM((tm, ATTN_OUT), BF16)],
        compiler_params=_cparams("parallel"),
        name="merge",
    )(y_ssd, *outs, *lses, pack, pack, h, w_a, w_b, w_o, nw)


FFN_TH = 256


def _ffn_kernel(u_ref, wgu_ref, wd_ref, h_ref, nw_ref, *rest, last):
    if last:
        out_ref, acc_ref = rest
    else:
        hout_ref, u_out_ref, acc_ref = rest
    acc_ref[...] = h_ref[...]
    u = u_ref[...]

    def hidden_step(j, c):
        lo = pl.multiple_of(j * FFN_TH, FFN_TH)
        gate = jnp.dot(u, wgu_ref[:, pl.ds(lo, FFN_TH)], preferred_element_type=F32)
        up = jnp.dot(u, wgu_ref[:, pl.ds(lo + FFN_HIDDEN, FFN_TH)], preferred_element_type=F32)
        act = (_silu(gate) * up).astype(BF16)
        acc_ref[...] += jnp.dot(act, wd_ref[pl.ds(lo, FFN_TH), :], preferred_element_type=F32)
        return c

    lax.fori_loop(0, FFN_HIDDEN // FFN_TH, hidden_step, 0)
    hn = acc_ref[...]
    normed = _rms_scale(hn) * nw_ref[...]
    if last:
        out_ref[...] = normed
    else:
        hout_ref[...] = hn
        u_out_ref[...] = normed.astype(u_out_ref.dtype)


def _ffn(u, w_gu, w_d, h, norm_w, last, tm=1024):
    t, d = u.shape
    rows = lambda: pl.BlockSpec((tm, d), lambda i: (i, 0))
    nw = norm_w.reshape(1, d)
    if last:
        out_specs = rows()
        out_shape = jax.ShapeDtypeStruct((t, d), F32)
    else:
        out_specs = [rows(), rows()]
        out_shape = [jax.ShapeDtypeStruct((t, d), F32), jax.ShapeDtypeStruct((t, d), BF16)]
    return pl.pallas_call(
        functools.partial(_ffn_kernel, last=last),
        grid=(t // tm,),
        in_specs=[rows(), _resident(w_gu), _resident(w_d), rows(), _resident(nw)],
        out_specs=out_specs,
        out_shape=out_shape,
        scratch_shapes=[pltpu.VMEM((tm, d), F32)],
        compiler_params=_cparams("parallel"),
        name="ffn_last" if last else "ffn",
    )(u, w_gu, w_d, h, nw)


def kernel(x, norm_mix, w_in, conv_w, conv_b, dt_bias, a_log, d_skip, ssd_norm, w_ssd_branch, w_attn_branch,
           w_out, norm_ffn, w_gate_up, w_down, norm_final):
    batch, seq, d = x.shape
    depth = w_in.shape[0]
    t = batch * seq
    tables = _rope_tables(seq)
    h = x.reshape(t, d)
    u = _rmsnorm(h, norm_mix[0])
    for layer in range(depth):
        w_pack, w_dt = _pack_in_weights(w_in, layer)
        pack, dt_raw = _inproj(u, w_pack, w_dt, tables)
        y_ssd = _ssd(pack, dt_raw, conv_w[layer], conv_b[layer], dt_bias[layer], a_log[layer], d_skip[layer],
                     ssd_norm[layer], batch, seq)
        assert [w // d_ for w, d_ in ATTN_PATTERNS] == [ATTN_BLOCK] * 3 and [d_ for _, d_ in ATTN_PATTERNS] == [1, 4, 16]
        pack3 = pack.reshape(batch, seq, PACK_W)
        outs, lses = zip(*[_attention_group(pack3, (PACK_Q + g, PACK_K, PACK_V), dilation)
                           for g, (_, dilation) in enumerate(ATTN_PATTERNS)])
        h, u_ffn = _merge(y_ssd, outs, lses, pack, h, w_ssd_branch[layer].astype(BF16),
                          w_attn_branch[layer].astype(BF16), w_out[layer].astype(BF16), norm_ffn[layer])
        last = layer == depth - 1
        nxt = norm_final if last else norm_mix[layer + 1]
        res = _ffn(u_ffn, w_gate_up[layer].astype(BF16), w_down[layer].astype(BF16), h, nxt, last)
        if last:
            return res.reshape(batch, seq, d)
        h, u = res
```

```python
import functools
import math

import jax
import jax.numpy as jnp
from jax import lax
from jax.experimental import pallas as pl
from jax.experimental.pallas import tpu as pltpu

F32 = jnp.float32
BF16 = jnp.bfloat16

D_MODEL = 1024
RMS_EPS = 1e-5

SSD_INNER = 2048
SSD_HEAD_DIM = 64
SSD_HEADS = 32
SSD_STATE = 128
SSD_GROUPS = 4
SSD_CONV = 4
SSD_CHUNK = 128
SSD_BC = 2 * SSD_GROUPS * SSD_STATE
SSD_CONV_CH = SSD_INNER + SSD_BC

ATTN_HEAD_DIM = 128
ATTN_HEADS = 8
ATTN_PATTERNS = ((128, 1), (512, 4), (2048, 16))
ATTN_BLOCK = 128
ROPE_THETA = 500000.0
ROPE_DIM = 32
ATTN_OUT = ATTN_HEADS * ATTN_HEAD_DIM

FFN_HIDDEN = 2816

PACK_W = 12 * 1024
PACK_Q, PACK_Z, PACK_K, PACK_V, PACK_GS, PACK_GA = 3, 6, 8, 9, 10, 11

Q_SCALE = ATTN_HEAD_DIM ** -0.5 * math.log2(math.e)
LN2 = math.log(2.0)

LANES = 128
NEG_BIG = -1e30
VMEM_LIMIT = 56 * 1024 * 1024


def _cparams(*sem):
    return pltpu.CompilerParams(dimension_semantics=sem, vmem_limit_bytes=VMEM_LIMIT)


def _sigmoid(x):
    return 1.0 / (1.0 + jnp.exp(-x))


def _silu(x):
    h = 0.5 * x
    return h + h * jnp.tanh(h)


def _softplus(x):
    return jnp.maximum(x, 0.0) + jnp.log(1.0 + jnp.exp(-jnp.abs(x)))


def _rms_scale(x):
    return x * lax.rsqrt(jnp.mean(x * x, axis=-1, keepdims=True) + RMS_EPS)


def _rmsnorm_kernel(x_ref, w_ref, o_ref):
    o_ref[...] = (_rms_scale(x_ref[...]) * w_ref[...]).astype(o_ref.dtype)


def _rmsnorm(x, w, tm=1024):
    t, d = x.shape
    return pl.pallas_call(
        _rmsnorm_kernel,
        grid=(t // tm,),
        in_specs=[pl.BlockSpec((tm, d), lambda i: (i, 0)), pl.BlockSpec((1, d), lambda i: (0, 0))],
        out_specs=pl.BlockSpec((tm, d), lambda i: (i, 0)),
        out_shape=jax.ShapeDtypeStruct((t, d), BF16),
        compiler_params=_cparams("parallel"),
        name="rmsnorm",
    )(x, w.reshape(1, d))


IN_DT_COL = SSD_INNER + SSD_CONV_CH
NT_DIMS = (((1,), (1,)), ((), ()))


def _pack_kernel(a_ref, o_ref):
    j = pl.program_id(0)
    rotary = (j >= PACK_Q) & (j < PACK_Z) | (j == PACK_K)
    half, mid = ROPE_DIM // 2, ATTN_HEAD_DIM // 2

    @pl.when(jnp.logical_not(rotary))
    def _():
        o_ref[...] = a_ref[0].astype(o_ref.dtype)

    @pl.when(rotary)
    def _():
        for h in range(a_ref.shape[1] // ATTN_HEAD_DIM):
            b = h * ATTN_HEAD_DIM
            for dst, src, n in ((0, 0, half), (half, mid, half), (2 * half, 2 * half, mid - 2 * half),
                                (mid, half, half), (mid + half, mid + half, mid - half)):
                o_ref[b + dst:b + dst + n, :] = a_ref[0, b + src:b + src + n, :].astype(o_ref.dtype)


def _pack_in_weights(w_in, layer):
    _, d, _ = w_in.shape
    unit = ATTN_OUT
    w_t = jnp.swapaxes(w_in, 1, 2)
    late = IN_DT_COL + SSD_HEADS
    g = SSD_HEADS
    ug, lg = unit // g, late // g
    src = lambda j: g * jnp.where(j < PACK_Q, SSD_INNER // g + j * ug,
                                  jnp.where(j < PACK_Z, lg + (j - PACK_Q) * ug,
                                            jnp.where(j < PACK_K, (j - PACK_Z) * ug, lg + (j - PACK_Q - 2) * ug)))
    packed = pl.pallas_call(
        _pack_kernel,
        grid=(PACK_W // unit,),
        in_specs=[pl.BlockSpec((pl.Element(1), pl.Element(unit), pl.Element(d)), lambda j: (layer, src(j), 0))],
        out_specs=pl.BlockSpec((unit, d), lambda j: (j, 0)),
        out_shape=jax.ShapeDtypeStruct((PACK_W, d), BF16),
        compiler_params=_cparams("arbitrary"),
        name="pack_w_in",
    )(w_t)
    w_dt = jnp.pad(w_t[layer, IN_DT_COL:late, :], ((0, LANES - SSD_HEADS), (0, 0)))
    return packed, w_dt


def _inproj_kernel(u_ref, w_ref, wdt_ref, cos_ref, sin_ref, o_ref, dt_ref):
    j = pl.program_id(1)
    is_q = (j >= PACK_Q) & (j < PACK_Q + len(ATTN_PATTERNS))
    rotary = is_q | (j == PACK_K)

    @pl.when(rotary)
    def _():
        res = lax.dot_general(u_ref[...], w_ref[...], NT_DIMS, preferred_element_type=F32)
        scale = jnp.where(is_q, Q_SCALE, 1.0)
        cos, sin = cos_ref[...] * scale, sin_ref[...] * scale
        for h in range(res.shape[1] // LANES):
            x = res[:, h * LANES:(h + 1) * LANES]
            o_ref[:, h * LANES:(h + 1) * LANES] = (x * cos + pltpu.roll(x, LANES // 2, 1) * sin).astype(o_ref.dtype)

    @pl.when(jnp.logical_not(rotary))
    def _():
        o_ref[...] = lax.dot_general(u_ref[...], w_ref[...], NT_DIMS, preferred_element_type=F32).astype(o_ref.dtype)

    @pl.when(j == 0)
    def _():
        dt_ref[...] = lax.dot_general(u_ref[...], wdt_ref[...].astype(BF16), NT_DIMS, preferred_element_type=F32)


def _inproj(u, w_pack, w_dt, tables, tm=2048):
    t, d = u.shape
    n = w_pack.shape[0]
    tn = ATTN_OUT
    seq = tables[0].shape[0]
    tab_spec = pl.BlockSpec((tm, LANES), lambda i, j: (i % (seq // tm), 0))
    return pl.pallas_call(
        _inproj_kernel,
        grid=(t // tm, n // tn),
        in_specs=[
            pl.BlockSpec((tm, d), lambda i, j: (i, 0)),
            pl.BlockSpec((tn, d), lambda i, j: (j, 0)),
            pl.BlockSpec((LANES, d), lambda i, j: (0, 0)),
            tab_spec, tab_spec,
        ],
        out_specs=[
            pl.BlockSpec((tm, tn), lambda i, j: (i, j)),
            pl.BlockSpec((tm, LANES), lambda i, j: (i, 0)),
        ],
        out_shape=[jax.ShapeDtypeStruct((t, n), BF16), jax.ShapeDtypeStruct((t, LANES), F32)],
        compiler_params=_cparams("parallel", "arbitrary"),
        name="inproj",
    )(u, w_pack, w_dt, *tables)


SSD_NB = 1


def _ssd_kernel(xbc_ref, z_ref, dt_ref, *refs):
    convw_ref, convb_ref, dtb_ref, alog_ref, dskip_ref, normw_ref = refs[:6]
    o_ref = refs[6]
    xbuf_ref, state_ref, conv_ref, y_ref, acum_ref, rowm_ref = refs[7:]
    for s in range(SSD_NB):
        _ssd_conv(xbc_ref.at[0, s], convw_ref, convb_ref, xbuf_ref.at[s], state_ref.at[s], conv_ref.at[s])
    for s in range(SSD_NB):
        _ssd_scan(z_ref.at[0, s], dt_ref.at[0, s], dtb_ref, alog_ref, dskip_ref, normw_ref, o_ref.at[0, s],
                  state_ref.at[s], conv_ref.at[s], y_ref.at[s], acum_ref.at[s], rowm_ref.at[s])


def _ssd_conv(xbc_ref, convw_ref, convb_ref, xbuf_ref, state_ref, conv_ref):
    q = SSD_CHUNK

    @pl.when(pl.program_id(1) == 0)
    def _():
        xbuf_ref[0:q, :] = jnp.zeros((q, SSD_CONV_CH), BF16)
        state_ref[...] = jnp.zeros_like(state_ref)

    xbuf_ref[q:2 * q, :] = xbc_ref[...]
    ti = lax.broadcasted_iota(jnp.int32, (3 * q, 2 * q), 0)
    si = lax.broadcasted_iota(jnp.int32, (3 * q, 2 * q), 1)
    tap = lax.shift_right_logical(ti, q.bit_length() - 1)
    shift = jnp.where(si == (ti & (q - 1)) + (q - 3) + tap, 1.0, 0.0).astype(BF16)
    cw = 512

    def conv_body(i, c):
        cs = pl.ds(pl.multiple_of(i * cw, cw), cw)
        xx = xbuf_ref[:, cs]
        sh = jnp.dot(shift, xx, preferred_element_type=F32)
        w = convw_ref[:, cs]
        acc = (convb_ref[:, cs] + w[3:4] * xx[q:2 * q].astype(F32) + w[0:1] * sh[0:q] + w[1:2] * sh[q:2 * q]
               + w[2:3] * sh[2 * q:3 * q])
        conv_ref[:, cs] = _silu(acc)
        return c

    lax.fori_loop(0, SSD_CONV_CH // cw, conv_body, 0, unroll=3)
    xbuf_ref[0:q, :] = xbuf_ref[q:2 * q, :]


def _ssd_scan(z_ref, dt_ref, dtb_ref, alog_ref, dskip_ref, normw_ref, o_ref, state_ref, conv_ref, y_ref,
              acum_ref, rowm_ref):
    q = SSD_CHUNK
    hpg = SSD_HEADS // SSD_GROUPS

    dt = _softplus(dt_ref[...] + dtb_ref[...])
    adt = dt * (-jnp.exp(alog_ref[...]))
    rowi = lax.broadcasted_iota(jnp.int32, (q, q), 0)
    coli = lax.broadcasted_iota(jnp.int32, (q, q), 1)
    acum = adt
    for s in (1, 2, 4, 8, 16, 32, 64):
        acum = acum + jnp.where(rowi >= s, pltpu.roll(acum, s, 0), 0.0)
    rowm_ref[...] = acum.T - jnp.log(dt.T)
    for g in range(SSD_GROUPS):
        acum_ref[g] = pltpu.roll(acum, (LANES - hpg * g) % LANES, 1)
    causal = rowi >= coli
    left = coli < SSD_HEAD_DIM

    def group_body(g, c):
        a_g = acum_ref[g]
        rm_g = rowm_ref[pl.ds(pl.multiple_of(g * hpg, hpg), hpg), :]
        b_g = conv_ref[:, pl.ds(pl.multiple_of(SSD_INNER + g * SSD_STATE, LANES), SSD_STATE)]
        c_g = conv_ref[:, pl.ds(pl.multiple_of(SSD_INNER + (SSD_GROUPS + g) * SSD_STATE, LANES), SSD_STATE)]
        c_gb = c_g.astype(BF16)
        cb = lax.dot_general(c_gb, b_g.astype(BF16), (((1,), (1,)), ((), ())), preferred_element_type=F32)
        b_t = b_g.T
        for pp in range(hpg // 2):
            ps = pl.ds(pl.multiple_of((g * (hpg // 2) + pp) * LANES, LANES), LANES)
            m_parts, s_parts, ecols, ealasts = [], [], [], []
            for j in (2 * pp, 2 * pp + 1):
                col = a_g[:, j:j + 1]
                rowm = rm_g[j:j + 1, :]
                alast = a_g[q - 1:q, j:j + 1]
                m_parts.append(cb * jnp.exp(jnp.where(causal, col - rowm, NEG_BIG)))
                s_parts.append(b_t * jnp.exp(alast - rowm))
                ecols.append(jnp.exp(col))
                ealasts.append(jnp.exp(alast))
            lhs = jnp.concatenate([jnp.concatenate(m_parts, axis=1), jnp.concatenate(s_parts, axis=1)],
                                  axis=0).astype(BF16)
            xp = conv_ref[:, ps]
            rhs = jnp.concatenate([jnp.where(left, xp, 0.0), jnp.where(left, 0.0, xp)], axis=0).astype(BF16)
            res = jnp.dot(lhs, rhs, preferred_element_type=F32)
            st_old = state_ref[:, ps]
            y_off = jnp.dot(c_gb, st_old.astype(BF16), preferred_element_type=F32)
            y_ref[:, ps] = res[0:q] + y_off * jnp.where(left, ecols[0], ecols[1]) + dskip_ref[:, ps] * xp
            state_ref[:, ps] = st_old * jnp.where(left, ealasts[0], ealasts[1]) + res[q:2 * q]
        return c

    lax.fori_loop(0, SSD_GROUPS, group_body, 0, unroll=4)

    gw = SSD_INNER // SSD_GROUPS

    def norm_body(g, c):
        gs = pl.ds(pl.multiple_of(g * gw, gw), gw)
        yg = y_ref[:, gs] * _silu(z_ref[:, gs].astype(F32))
        o_ref[:, gs] = (_rms_scale(yg) * normw_ref[:, gs]).astype(o_ref.dtype)
        return c

    lax.fori_loop(0, SSD_GROUPS, norm_body, 0, unroll=4)


def _ssd(pack, dt_raw, conv_w, conv_b, dt_bias, a_log, d_skip, norm_w, batch, seq):
    t = pack.shape[0]
    nc = seq // SSD_CHUNK
    q = SSD_CHUNK
    pad = LANES - SSD_HEADS
    nb = SSD_NB
    const = lambda shape: pl.BlockSpec(shape, lambda b, c: (0, 0))
    rows = lambda width, col: pl.BlockSpec((1, nb, q, width), lambda b, c: (b, 0, c, col))
    pack4 = pack.reshape(batch // nb, nb, seq, PACK_W)
    out = pl.pallas_call(
        _ssd_kernel,
        grid=(batch // nb, nc),
        in_specs=[
            rows(SSD_CONV_CH, 0), rows(SSD_INNER, PACK_Z // 2), rows(LANES, 0),
            const((SSD_CONV, SSD_CONV_CH)), const((1, SSD_CONV_CH)), const((1, LANES)), const((1, LANES)),
            const((1, SSD_INNER)), const((1, SSD_INNER)),
        ],
        out_specs=rows(SSD_INNER, 0),
        out_shape=jax.ShapeDtypeStruct((batch // nb, nb, seq, SSD_INNER), BF16),
        scratch_shapes=[
            pltpu.VMEM((nb, 2 * q, SSD_CONV_CH), BF16),
            pltpu.VMEM((nb, SSD_STATE, SSD_INNER), F32),
            pltpu.VMEM((nb, q, SSD_CONV_CH), F32),
            pltpu.VMEM((nb, q, SSD_INNER), F32),
            pltpu.VMEM((nb, SSD_GROUPS, q, LANES), F32),
            pltpu.VMEM((nb, LANES, q), F32),
        ],
        compiler_params=_cparams("parallel", "arbitrary"),
        name="ssd",
    )(pack4, pack4, dt_raw.reshape(batch // nb, nb, seq, LANES), conv_w, conv_b.reshape(1, -1),
      jnp.pad(dt_bias, (0, pad)).reshape(1, LANES), jnp.pad(a_log, (0, pad)).reshape(1, LANES),
      jnp.repeat(d_skip, SSD_HEAD_DIM).reshape(1, SSD_INNER), norm_w.reshape(1, SSD_INNER))
    return out.reshape(t, SSD_INNER)


ATTN_HB = 4


def _attn_kernel(q_ref, k_ref, v_ref, o_ref, lse_ref, bias_ref, *scratch, dilation):
    assert dilation in (1, 4, 16)
    blk = ATTN_BLOCK
    seq = q_ref.shape[1]
    length = seq // dilation
    nblk = length // blk
    quarter = seq // 4
    direct = dilation == 1
    if not direct:
        nat_ref, tmp_ref, qrm_ref, krm_ref, vrm_ref, orm_ref, lrm_ref = scratch
    hsl = lambda h: slice(h * LANES, (h + 1) * LANES)

    def split_rows(src_ref, dst_ref, h):
        nat_ref[...] = src_ref[0, :, hsl(h)].astype(F32)
        if dilation == 4:
            for r in range(4):
                dst_ref[r * quarter:(r + 1) * quarter, hsl(h)] = (
                    nat_ref[pl.ds(r, quarter, stride=4), :].astype(BF16))
        else:
            for r in range(4):
                tmp_ref[r * quarter:(r + 1) * quarter, :] = nat_ref[pl.ds(r, quarter, stride=4), :]
            for r in range(4):
                for r2 in range(4):
                    c = r * 4 + r2
                    dst_ref[c * length:(c + 1) * length, hsl(h)] = (
                        tmp_ref[pl.ds(r * quarter + r2, length, stride=4), :].astype(BF16))

    def merge_rows(src):
        if dilation == 4:
            for r in range(4):
                nat_ref[pl.ds(r, quarter, stride=4), :] = src[r * quarter:(r + 1) * quarter, :]
        else:
            for r in range(4):
                for r2 in range(4):
                    c = r * 4 + r2
                    tmp_ref[pl.ds(r * quarter + r2, length, stride=4), :] = src[c * length:(c + 1) * length, :]
            for r in range(4):
                nat_ref[pl.ds(r, quarter, stride=4), :] = tmp_ref[r * quarter:(r + 1) * quarter, :]

    if direct:
        rows = lambda ref, h, r0, n: ref[0, pl.ds(r0, n), hsl(h)]
        qs, ks, vs = q_ref, k_ref, v_ref
    else:
        for h in range(ATTN_HB):
            split_rows(q_ref, qrm_ref, h)
            split_rows(k_ref, krm_ref, h)
            split_rows(v_ref, vrm_ref, h)
        rows = lambda ref, h, r0, n: ref[pl.ds(r0, n), hsl(h)]
        qs, ks, vs = qrm_ref, krm_ref, vrm_ref

    qi = lax.broadcasted_iota(jnp.int32, (blk, 2 * blk), 0)
    kj = lax.broadcasted_iota(jnp.int32, (blk, 2 * blk), 1)
    lane = lax.broadcasted_iota(jnp.int32, (blk, LANES), 1)
    bias_ref[0] = jnp.where(kj <= qi, 0.0, NEG_BIG)
    bias_ref[1] = jnp.where((kj >= qi) & (kj <= qi + blk), 0.0, NEG_BIG)
    bias_ref[2] = jnp.where((kj >= blk) & (kj <= qi + blk), 0.0, NEG_BIG)

    def block(i, c):
        r0 = pl.multiple_of(i * blk, blk)
        k0 = pl.multiple_of(jnp.maximum(r0 - blk, 0), blk)
        bias = bias_ref[jnp.where(i == 0, 0, jnp.where(i % nblk == 0, 2, 1))]
        lse_tile = jnp.zeros((blk, LANES), F32)
        for h in range(ATTN_HB):
            s = lax.dot_general(rows(qs, h, r0, blk), rows(ks, h, k0, 2 * blk), NT_DIMS, preferred_element_type=F32)
            s = s + bias
            m = jnp.max(s, axis=-1, keepdims=True)
            p = jnp.exp2(s - m).astype(BF16)
            vx = jnp.concatenate([rows(vs, h, k0, 2 * blk), jnp.ones((2 * blk, LANES), BF16)], axis=1)
            r = jnp.dot(p, vx, preferred_element_type=F32)
            l = r[:, LANES:]
            o = r[:, :LANES] * (1.0 / l)
            if direct:
                o_ref[0, pl.ds(r0, blk), hsl(h)] = o.astype(o_ref.dtype)
            else:
                orm_ref[h, pl.ds(r0, blk), :] = o
            lse_tile = jnp.where(lane == h, m + jnp.log(l) * (1.0 / LN2), lse_tile)
        if direct:
            lse_ref[0, pl.ds(r0, blk), :] = lse_tile
        else:
            lrm_ref[pl.ds(r0, blk), :] = lse_tile
        return c

    lax.fori_loop(0, seq // blk, block, 0, unroll=8)

    if not direct:
        for h in range(ATTN_HB):
            merge_rows(orm_ref.at[h])
            o_ref[0, :, hsl(h)] = nat_ref[...].astype(o_ref.dtype)
        merge_rows(lrm_ref)
        lse_ref[0] = nat_ref[...]


def _attention_group(src3, units, dilation):
    batch, seq, _ = src3.shape
    halves = ATTN_HEADS // ATTN_HB
    wb = ATTN_HB * ATTN_HEAD_DIM
    col = lambda unit: (lambda b, hh: (b, 0, unit * halves + hh))
    scratch = [] if dilation == 1 else [
        pltpu.VMEM((seq, LANES), F32),
        pltpu.VMEM((seq, LANES), F32),
        pltpu.VMEM((seq, wb), BF16),
        pltpu.VMEM((seq, wb), BF16),
        pltpu.VMEM((seq, wb), BF16),
        pltpu.VMEM((ATTN_HB, seq, LANES), F32),
        pltpu.VMEM((seq, LANES), F32),
    ]
    out, lse = pl.pallas_call(
        functools.partial(_attn_kernel, dilation=dilation),
        grid=(batch, halves),
        in_specs=[pl.BlockSpec((1, seq, wb), col(u)) for u in units],
        out_specs=[
            pl.BlockSpec((1, seq, wb), lambda b, hh: (b, 0, hh)),
            pl.BlockSpec((1, seq, LANES), lambda b, hh: (b, 0, hh)),
        ],
        out_shape=[
            jax.ShapeDtypeStruct((batch, seq, ATTN_OUT), BF16),
            jax.ShapeDtypeStruct((batch, seq, halves * LANES), F32),
        ],
        scratch_shapes=[pltpu.VMEM((3, ATTN_BLOCK, 2 * ATTN_BLOCK), F32)] + scratch,
        compiler_params=_cparams("parallel", "parallel"),
        name=f"attn_d{dilation}",
    )(src3, src3, src3)
    return out.reshape(batch * seq, ATTN_OUT), lse.reshape(batch * seq, halves * LANES)


def _rope_tables(seq):
    half = ROPE_DIM // 2
    inv = ROPE_THETA ** (-jnp.arange(0, ROPE_DIM, 2, dtype=F32) / ROPE_DIM)
    ang = jnp.arange(seq, dtype=F32)[:, None] * inv[None, :]
    c, s = jnp.cos(ang), jnp.sin(ang)
    gap = LANES // 2 - half
    one, zero = jnp.ones((seq, gap), F32), jnp.zeros((seq, gap), F32)
    cos = jnp.concatenate([c, one, c, one], axis=1)
    sin = jnp.concatenate([-s, zero, s, zero], axis=1)
    return cos, sin


def _merge_kernel(yssd_ref, o0_ref, o1_ref, o2_ref, l0_ref, l1_ref, l2_ref, gs_ref, ga_ref, h_ref,
                  wa_ref, wb_ref, wo_ref, nw_ref, hout_ref, u_ref, yattn_ref):
    l0, l1, l2 = l0_ref[...], l1_ref[...], l2_ref[...]
    m = jnp.maximum(jnp.maximum(l0, l1), l2)
    e0, e1, e2 = jnp.exp2(l0 - m), jnp.exp2(l1 - m), jnp.exp2(l2 - m)
    inv = 1.0 / (e0 + e1 + e2)
    w0, w1, w2 = e0 * inv, e1 * inv, e2 * inv
    for h in range(ATTN_HEADS):
        sl = slice(h * LANES, (h + 1) * LANES)
        c = (h // ATTN_HB) * LANES + h % ATTN_HB
        yh = (w0[:, c:c + 1] * o0_ref[:, sl].astype(F32) + w1[:, c:c + 1] * o1_ref[:, sl].astype(F32)
              + w2[:, c:c + 1] * o2_ref[:, sl].astype(F32))
        yattn_ref[:, sl] = yh.astype(BF16)
    a = jnp.dot(yssd_ref[...], wa_ref[...], preferred_element_type=F32)
    b = jnp.dot(yattn_ref[...], wb_ref[...], preferred_element_type=F32)
    merged = _sigmoid(gs_ref[...].astype(F32)) * a + _sigmoid(ga_ref[...].astype(F32)) * b
    hn = h_ref[...] + jnp.dot(merged.astype(BF16), wo_ref[...], preferred_element_type=F32)
    hout_ref[...] = hn
    u_ref[...] = (_rms_scale(hn) * nw_ref[...]).astype(u_ref.dtype)


def _resident(a):
    return pl.BlockSpec(a.shape, lambda *_: (0,) * a.ndim, pipeline_mode=pl.Buffered(1))


def _merge(y_ssd, outs, lses, pack, h, w_a, w_b, w_o, norm_w, tm=512):
    t = h.shape[0]
    d = D_MODEL
    rows = lambda w, col=0: pl.BlockSpec((tm, w), lambda i, col=col: (i, col))
    nw = norm_w.reshape(1, d)
    lw = lses[0].shape[1]
    return pl.pallas_call(
        _merge_kernel,
        grid=(t // tm,),
        in_specs=[rows(SSD_INNER), rows(d), rows(d), rows(d), rows(lw), rows(lw), rows(lw),
                  rows(d, PACK_GS), rows(d, PACK_GA), rows(d), _resident(w_a), _resident(w_b), _resident(w_o),
                  _resident(nw)],
        out_specs=[rows(d), rows(d)],
        out_shape=[jax.ShapeDtypeStruct((t, d), F32), jax.ShapeDtypeStruct((t, d), BF16)],
        scratch_shapes=[pltpu.VMEM((tm, ATTN_OUT), BF16)],
        compiler_params=_cparams("parallel"),
        name="merge",
    )(y_ssd, *outs, *lses, pack, pack, h, w_a, w_b, w_o, nw)


FFN_TH = 256


def _ffn_kernel(u_ref, wgu_ref, wd_ref, h_ref, nw_ref, *rest, last):
    if last:
        out_ref, acc_ref = rest
    else:
        hout_ref, u_out_ref, acc_ref = rest
    acc_ref[...] = h_ref[...]
    u = u_ref[...]

    def hidden_step(j, c):
        lo = pl.multiple_of(j * FFN_TH, FFN_TH)
        gate = jnp.dot(u, wgu_ref[:, pl.ds(lo, FFN_TH)], preferred_element_type=F32)
        up = jnp.dot(u, wgu_ref[:, pl.ds(lo + FFN_HIDDEN, FFN_TH)], preferred_element_type=F32)
        act = (_silu(gate) * up).astype(BF16)
        acc_ref[...] += jnp.dot(act, wd_ref[pl.ds(lo, FFN_TH), :], preferred_element_type=F32)
        return c

    lax.fori_loop(0, FFN_HIDDEN // FFN_TH, hidden_step, 0, unroll=True)
    hn = acc_ref[...]
    normed = _rms_scale(hn) * nw_ref[...]
    if last:
        out_ref[...] = normed
    else:
        hout_ref[...] = hn
        u_out_ref[...] = normed.astype(u_out_ref.dtype)


def _ffn(u, w_gu, w_d, h, norm_w, last, tm=1024):
    t, d = u.shape
    rows = lambda: pl.BlockSpec((tm, d), lambda i: (i, 0))
    nw = norm_w.reshape(1, d)
    if last:
        out_specs = rows()
        out_shape = jax.ShapeDtypeStruct((t, d), F32)
    else:
        out_specs = [rows(), rows()]
        out_shape = [jax.ShapeDtypeStruct((t, d), F32), jax.ShapeDtypeStruct((t, d), BF16)]
    return pl.pallas_call(
        functools.partial(_ffn_kernel, last=last),
        grid=(t // tm,),
        in_specs=[rows(), _resident(w_gu), _resident(w_d), rows(), _resident(nw)],
        out_specs=out_specs,
        out_shape=out_shape,
        scratch_shapes=[pltpu.VMEM((tm, d), F32)],
        compiler_params=_cparams("parallel"),
        name="ffn_last" if last else "ffn",
    )(u, w_gu, w_d, h, nw)


def kernel(x, norm_mix, w_in, conv_w, conv_b, dt_bias, a_log, d_skip, ssd_norm, w_ssd_branch, w_attn_branch,
           w_out, norm_ffn, w_gate_up, w_down, norm_final):
    batch, seq, d = x.shape
    depth = w_in.shape[0]
    t = batch * seq
    tables = _rope_tables(seq)
    h = x.reshape(t, d)
    u = _rmsnorm(h, norm_mix[0])
    for layer in range(depth):
        w_pack, w_dt = _pack_in_weights(w_in, layer)
        pack, dt_raw = _inproj(u, w_pack, w_dt, tables)
        y_ssd = _ssd(pack, dt_raw, conv_w[layer], conv_b[layer], dt_bias[layer], a_log[layer], d_skip[layer],
                     ssd_norm[layer], batch, seq)
        assert [w // d_ for w, d_ in ATTN_PATTERNS] == [ATTN_BLOCK] * 3 and [d_ for _, d_ in ATTN_PATTERNS] == [1, 4, 16]
        pack3 = pack.reshape(batch, seq, PACK_W)
        outs, lses = zip(*[_attention_group(pack3, (PACK_Q + g, PACK_K, PACK_V), dilation)
                           for g, (_, dilation) in enumerate(ATTN_PATTERNS)])
        h, u_ffn = _merge(y_ssd, outs, lses, pack, h, w_ssd_branch[layer].astype(BF16),
                          w_attn_branch[layer].astype(BF16), w_out[layer].astype(BF16), norm_ffn[layer])
        last = layer == depth - 1
        nxt = norm_final if last else norm_mix[layer + 1]
        res = _ffn(u_ffn, w_gate_up[layer].astype(BF16), w_down[layer].astype(BF16), h, nxt, last)
        if last:
            return res.reshape(batch, seq, d)
        h, u = res
```

```python
import functools
import math

import jax
import jax.numpy as jnp
from jax import lax
from jax.experimental import pallas as pl
from jax.experimental.pallas import tpu as pltpu

F32 = jnp.float32
BF16 = jnp.bfloat16

D_MODEL = 1024
RMS_EPS = 1e-5

SSD_INNER = 2048
SSD_HEAD_DIM = 64
SSD_HEADS = 32
SSD_STATE = 128
SSD_GROUPS = 4
SSD_CONV = 4
SSD_CHUNK = 128
SSD_BC = 2 * SSD_GROUPS * SSD_STATE
SSD_CONV_CH = SSD_INNER + SSD_BC

ATTN_HEAD_DIM = 128
ATTN_HEADS = 8
ATTN_PATTERNS = ((128, 1), (512, 4), (2048, 16))
ATTN_BLOCK = 128
ROPE_THETA = 500000.0
ROPE_DIM = 32
ATTN_OUT = ATTN_HEADS * ATTN_HEAD_DIM

FFN_HIDDEN = 2816

PACK_W = 12 * 1024
PACK_Q, PACK_Z, PACK_K, PACK_V, PACK_GS, PACK_GA = 3, 6, 8, 9, 10, 11

Q_SCALE = ATTN_HEAD_DIM ** -0.5 * math.log2(math.e)
LN2 = math.log(2.0)

LANES = 128
NEG_BIG = -1e30
VMEM_LIMIT = 56 * 1024 * 1024


def _cparams(*sem):
    return pltpu.CompilerParams(dimension_semantics=sem, vmem_limit_bytes=VMEM_LIMIT)


def _sigmoid(x):
    return 1.0 / (1.0 + jnp.exp(-x))


def _silu(x):
    h = 0.5 * x
    return h + h * jnp.tanh(h)


def _softplus(x):
    return jnp.maximum(x, 0.0) + jnp.log(1.0 + jnp.exp(-jnp.abs(x)))


def _rms_scale(x):
    return x * lax.rsqrt(jnp.mean(x * x, axis=-1, keepdims=True) + RMS_EPS)


def _rmsnorm_kernel(x_ref, w_ref, o_ref):
    o_ref[...] = (_rms_scale(x_ref[...]) * w_ref[...]).astype(o_ref.dtype)


def _rmsnorm(x, w, tm=1024):
    t, d = x.shape
    return pl.pallas_call(
        _rmsnorm_kernel,
        grid=(t // tm,),
        in_specs=[pl.BlockSpec((tm, d), lambda i: (i, 0)), pl.BlockSpec((1, d), lambda i: (0, 0))],
        out_specs=pl.BlockSpec((tm, d), lambda i: (i, 0)),
        out_shape=jax.ShapeDtypeStruct((t, d), BF16),
        compiler_params=_cparams("parallel"),
        name="rmsnorm",
    )(x, w.reshape(1, d))


IN_DT_COL = SSD_INNER + SSD_CONV_CH
NT_DIMS = (((1,), (1,)), ((), ()))


def _pack_kernel(a_ref, o_ref):
    j = pl.program_id(0)
    rotary = (j >= PACK_Q) & (j < PACK_Z) | (j == PACK_K)
    half, mid = ROPE_DIM // 2, ATTN_HEAD_DIM // 2

    @pl.when(jnp.logical_not(rotary))
    def _():
        o_ref[...] = a_ref[0].astype(o_ref.dtype)

    @pl.when(rotary)
    def _():
        for h in range(a_ref.shape[1] // ATTN_HEAD_DIM):
            b = h * ATTN_HEAD_DIM
            for dst, src, n in ((0, 0, half), (half, mid, half), (2 * half, 2 * half, mid - 2 * half),
                                (mid, half, half), (mid + half, mid + half, mid - half)):
                o_ref[b + dst:b + dst + n, :] = a_ref[0, b + src:b + src + n, :].astype(o_ref.dtype)


def _pack_in_weights(w_in, layer):
    _, d, _ = w_in.shape
    unit = ATTN_OUT
    w_t = jnp.swapaxes(w_in, 1, 2)
    late = IN_DT_COL + SSD_HEADS
    g = SSD_HEADS
    ug, lg = unit // g, late // g
    src = lambda j: g * jnp.where(j < PACK_Q, SSD_INNER // g + j * ug,
                                  jnp.where(j < PACK_Z, lg + (j - PACK_Q) * ug,
                                            jnp.where(j < PACK_K, (j - PACK_Z) * ug, lg + (j - PACK_Q - 2) * ug)))
    packed = pl.pallas_call(
        _pack_kernel,
        grid=(PACK_W // unit,),
        in_specs=[pl.BlockSpec((pl.Element(1), pl.Element(unit), pl.Element(d)), lambda j: (layer, src(j), 0))],
        out_specs=pl.BlockSpec((unit, d), lambda j: (j, 0)),
        out_shape=jax.ShapeDtypeStruct((PACK_W, d), BF16),
        compiler_params=_cparams("arbitrary"),
        name="pack_w_in",
    )(w_t)
    w_dt = jnp.pad(w_t[layer, IN_DT_COL:late, :], ((0, LANES - SSD_HEADS), (0, 0)))
    return packed, w_dt


def _inproj_kernel(u_ref, w_ref, wdt_ref, cos_ref, sin_ref, o_ref, dt_ref):
    j = pl.program_id(1)
    is_q = (j >= PACK_Q) & (j < PACK_Q + len(ATTN_PATTERNS))
    rotary = is_q | (j == PACK_K)

    @pl.when(rotary)
    def _():
        res = lax.dot_general(u_ref[...], w_ref[...], NT_DIMS, preferred_element_type=F32)
        scale = jnp.where(is_q, Q_SCALE, 1.0)
        cos, sin = cos_ref[...] * scale, sin_ref[...] * scale
        for h in range(res.shape[1] // LANES):
            x = res[:, h * LANES:(h + 1) * LANES]
            o_ref[:, h * LANES:(h + 1) * LANES] = (x * cos + pltpu.roll(x, LANES // 2, 1) * sin).astype(o_ref.dtype)

    @pl.when(jnp.logical_not(rotary))
    def _():
        o_ref[...] = lax.dot_general(u_ref[...], w_ref[...], NT_DIMS, preferred_element_type=F32).astype(o_ref.dtype)

    @pl.when(j == 0)
    def _():
        dt_ref[...] = lax.dot_general(u_ref[...], wdt_ref[...].astype(BF16), NT_DIMS, preferred_element_type=F32)


def _inproj(u, w_pack, w_dt, tables, tm=2048):
    t, d = u.shape
    n = w_pack.shape[0]
    tn = ATTN_OUT
    seq = tables[0].shape[0]
    tab_spec = pl.BlockSpec((tm, LANES), lambda i, j: (i % (seq // tm), 0))
    return pl.pallas_call(
        _inproj_kernel,
        grid=(t // tm, n // tn),
        in_specs=[
            pl.BlockSpec((tm, d), lambda i, j: (i, 0)),
            pl.BlockSpec((tn, d), lambda i, j: (j, 0)),
            pl.BlockSpec((LANES, d), lambda i, j: (0, 0)),
            tab_spec, tab_spec,
        ],
        out_specs=[
            pl.BlockSpec((tm, tn), lambda i, j: (i, j)),
            pl.BlockSpec((tm, LANES), lambda i, j: (i, 0)),
        ],
        out_shape=[jax.ShapeDtypeStruct((t, n), BF16), jax.ShapeDtypeStruct((t, LANES), F32)],
        compiler_params=_cparams("parallel", "arbitrary"),
        name="inproj",
    )(u, w_pack, w_dt, *tables)


SSD_NB = 1


def _ssd_kernel(xbc_ref, z_ref, dt_ref, *refs):
    convw_ref, convb_ref, dtb_ref, alog_ref, dskip_ref, normw_ref = refs[:6]
    o_ref = refs[6]
    xbuf_ref, state_ref, conv_ref, y_ref, acum_ref, rowm_ref = refs[7:]
    for s in range(SSD_NB):
        _ssd_conv(xbc_ref.at[0, s], convw_ref, convb_ref, xbuf_ref.at[s], state_ref.at[s], conv_ref.at[s])
    for s in range(SSD_NB):
        _ssd_scan(z_ref.at[0, s], dt_ref.at[0, s], dtb_ref, alog_ref, dskip_ref, normw_ref, o_ref.at[0, s],
                  state_ref.at[s], conv_ref.at[s], y_ref.at[s], acum_ref.at[s], rowm_ref.at[s])


def _ssd_conv(xbc_ref, convw_ref, convb_ref, xbuf_ref, state_ref, conv_ref):
    q = SSD_CHUNK

    @pl.when(pl.program_id(1) == 0)
    def _():
        xbuf_ref[0:q, :] = jnp.zeros((q, SSD_CONV_CH), BF16)
        state_ref[...] = jnp.zeros_like(state_ref)

    xbuf_ref[q:2 * q, :] = xbc_ref[...]
    ti = lax.broadcasted_iota(jnp.int32, (3 * q, 2 * q), 0)
    si = lax.broadcasted_iota(jnp.int32, (3 * q, 2 * q), 1)
    tap = lax.shift_right_logical(ti, q.bit_length() - 1)
    shift = jnp.where(si == (ti & (q - 1)) + (q - 3) + tap, 1.0, 0.0).astype(BF16)
    cw = 512

    def conv_body(i, c):
        cs = pl.ds(pl.multiple_of(i * cw, cw), cw)
        xx = xbuf_ref[:, cs]
        sh = jnp.dot(shift, xx, preferred_element_type=F32)
        w = convw_ref[:, cs]
        acc = (convb_ref[:, cs] + w[3:4] * xx[q:2 * q].astype(F32) + w[0:1] * sh[0:q] + w[1:2] * sh[q:2 * q]
               + w[2:3] * sh[2 * q:3 * q])
        conv_ref[:, cs] = _silu(acc)
        return c

    lax.fori_loop(0, SSD_CONV_CH // cw, conv_body, 0, unroll=True)
    xbuf_ref[0:q, :] = xbuf_ref[q:2 * q, :]


def _ssd_scan(z_ref, dt_ref, dtb_ref, alog_ref, dskip_ref, normw_ref, o_ref, state_ref, conv_ref, y_ref,
              acum_ref, rowm_ref):
    q = SSD_CHUNK
    hpg = SSD_HEADS // SSD_GROUPS

    dt = _softplus(dt_ref[...] + dtb_ref[...])
    adt = dt * (-jnp.exp(alog_ref[...]))
    rowi = lax.broadcasted_iota(jnp.int32, (q, q), 0)
    coli = lax.broadcasted_iota(jnp.int32, (q, q), 1)
    acum = adt
    for s in (1, 2, 4, 8, 16, 32, 64):
        acum = acum + jnp.where(rowi >= s, pltpu.roll(acum, s, 0), 0.0)
    rowm_ref[...] = acum.T - jnp.log(dt.T)
    for g in range(SSD_GROUPS):
        acum_ref[g] = pltpu.roll(acum, (LANES - hpg * g) % LANES, 1)
    causal = rowi >= coli
    left = coli < SSD_HEAD_DIM

    def group_body(g, c):
        a_g = acum_ref[g]
        rm_g = rowm_ref[pl.ds(pl.multiple_of(g * hpg, hpg), hpg), :]
        b_g = conv_ref[:, pl.ds(pl.multiple_of(SSD_INNER + g * SSD_STATE, LANES), SSD_STATE)]
        c_g = conv_ref[:, pl.ds(pl.multiple_of(SSD_INNER + (SSD_GROUPS + g) * SSD_STATE, LANES), SSD_STATE)]
        c_gb = c_g.astype(BF16)
        cb = lax.dot_general(c_gb, b_g.astype(BF16), (((1,), (1,)), ((), ())), preferred_element_type=F32)
        b_t = b_g.T
        for pp in range(hpg // 2):
            ps = pl.ds(pl.multiple_of((g * (hpg // 2) + pp) * LANES, LANES), LANES)
            m_parts, s_parts, ecols, ealasts = [], [], [], []
            for j in (2 * pp, 2 * pp + 1):
                col = a_g[:, j:j + 1]
                rowm = rm_g[j:j + 1, :]
                alast = a_g[q - 1:q, j:j + 1]
                m_parts.append(cb * jnp.exp(jnp.where(causal, col - rowm, NEG_BIG)))
                s_parts.append(b_t * jnp.exp(alast - rowm))
                ecols.append(jnp.exp(col))
                ealasts.append(jnp.exp(alast))
            lhs = jnp.concatenate([jnp.concatenate(m_parts, axis=1), jnp.concatenate(s_parts, axis=1)],
                                  axis=0).astype(BF16)
            xp = conv_ref[:, ps]
            rhs = jnp.concatenate([jnp.where(left, xp, 0.0), jnp.where(left, 0.0, xp)], axis=0).astype(BF16)
            res = jnp.dot(lhs, rhs, preferred_element_type=F32)
            st_old = state_ref[:, ps]
            y_off = jnp.dot(c_gb, st_old.astype(BF16), preferred_element_type=F32)
            y_ref[:, ps] = res[0:q] + y_off * jnp.where(left, ecols[0], ecols[1]) + dskip_ref[:, ps] * xp
            state_ref[:, ps] = st_old * jnp.where(left, ealasts[0], ealasts[1]) + res[q:2 * q]
        return c

    lax.fori_loop(0, SSD_GROUPS, group_body, 0, unroll=4)

    gw = SSD_INNER // SSD_GROUPS

    def norm_body(g, c):
        gs = pl.ds(pl.multiple_of(g * gw, gw), gw)
        yg = y_ref[:, gs] * _silu(z_ref[:, gs].astype(F32))
        o_ref[:, gs] = (_rms_scale(yg) * normw_ref[:, gs]).astype(o_ref.dtype)
        return c

    lax.fori_loop(0, SSD_GROUPS, norm_body, 0, unroll=4)


def _ssd(pack, dt_raw, conv_w, conv_b, dt_bias, a_log, d_skip, norm_w, batch, seq):
    t = pack.shape[0]
    nc = seq // SSD_CHUNK
    q = SSD_CHUNK
    pad = LANES - SSD_HEADS
    nb = SSD_NB
    const = lambda shape: pl.BlockSpec(shape, lambda b, c: (0, 0))
    rows = lambda width, col: pl.BlockSpec((1, nb, q, width), lambda b, c: (b, 0, c, col))
    pack4 = pack.reshape(batch // nb, nb, seq, PACK_W)
    out = pl.pallas_call(
        _ssd_kernel,
        grid=(batch // nb, nc),
        in_specs=[
            rows(SSD_CONV_CH, 0), rows(SSD_INNER, PACK_Z // 2), rows(LANES, 0),
            const((SSD_CONV, SSD_CONV_CH)), const((1, SSD_CONV_CH)), const((1, LANES)), const((1, LANES)),
            const((1, SSD_INNER)), const((1, SSD_INNER)),
        ],
        out_specs=rows(SSD_INNER, 0),
        out_shape=jax.ShapeDtypeStruct((batch // nb, nb, seq, SSD_INNER), BF16),
        scratch_shapes=[
            pltpu.VMEM((nb, 2 * q, SSD_CONV_CH), BF16),
            pltpu.VMEM((nb, SSD_STATE, SSD_INNER), F32),
            pltpu.VMEM((nb, q, SSD_CONV_CH), F32),
            pltpu.VMEM((nb, q, SSD_INNER), F32),
            pltpu.VMEM((nb, SSD_GROUPS, q, LANES), F32),
            pltpu.VMEM((nb, LANES, q), F32),
        ],
        compiler_params=_cparams("parallel", "arbitrary"),
        name="ssd",
    )(pack4, pack4, dt_raw.reshape(batch // nb, nb, seq, LANES), conv_w, conv_b.reshape(1, -1),
      jnp.pad(dt_bias, (0, pad)).reshape(1, LANES), jnp.pad(a_log, (0, pad)).reshape(1, LANES),
      jnp.repeat(d_skip, SSD_HEAD_DIM).reshape(1, SSD_INNER), norm_w.reshape(1, SSD_INNER))
    return out.reshape(t, SSD_INNER)


ATTN_HB = 4


def _attn_kernel(q_ref, k_ref, v_ref, o_ref, lse_ref, bias_ref, *scratch, dilation):
    assert dilation in (1, 4, 16)
    blk = ATTN_BLOCK
    seq = q_ref.shape[1]
    length = seq // dilation
    nblk = length // blk
    quarter = seq // 4
    direct = dilation == 1
    if not direct:
        nat_ref, tmp_ref, qrm_ref, krm_ref, vrm_ref, orm_ref, lrm_ref = scratch
    hsl = lambda h: slice(h * LANES, (h + 1) * LANES)

    def split_rows(src_ref, dst_ref, h):
        nat_ref[...] = src_ref[0, :, hsl(h)].astype(F32)
        if dilation == 4:
            for r in range(4):
                dst_ref[r * quarter:(r + 1) * quarter, hsl(h)] = (
                    nat_ref[pl.ds(r, quarter, stride=4), :].astype(BF16))
        else:
            for r in range(4):
                tmp_ref[r * quarter:(r + 1) * quarter, :] = nat_ref[pl.ds(r, quarter, stride=4), :]
            for r in range(4):
                for r2 in range(4):
                    c = r * 4 + r2
                    dst_ref[c * length:(c + 1) * length, hsl(h)] = (
                        tmp_ref[pl.ds(r * quarter + r2, length, stride=4), :].astype(BF16))

    def merge_rows(src):
        if dilation == 4:
            for r in range(4):
                nat_ref[pl.ds(r, quarter, stride=4), :] = src[r * quarter:(r + 1) * quarter, :]
        else:
            for r in range(4):
                for r2 in range(4):
                    c = r * 4 + r2
                    tmp_ref[pl.ds(r * quarter + r2, length, stride=4), :] = src[c * length:(c + 1) * length, :]
            for r in range(4):
                nat_ref[pl.ds(r, quarter, stride=4), :] = tmp_ref[r * quarter:(r + 1) * quarter, :]

    if direct:
        rows = lambda ref, h, r0, n: ref[0, pl.ds(r0, n), hsl(h)]
        qs, ks, vs = q_ref, k_ref, v_ref
    else:
        for h in range(ATTN_HB):
            split_rows(q_ref, qrm_ref, h)
            split_rows(k_ref, krm_ref, h)
            split_rows(v_ref, vrm_ref, h)
        rows = lambda ref, h, r0, n: ref[pl.ds(r0, n), hsl(h)]
        qs, ks, vs = qrm_ref, krm_ref, vrm_ref

    qi = lax.broadcasted_iota(jnp.int32, (blk, 2 * blk), 0)
    kj = lax.broadcasted_iota(jnp.int32, (blk, 2 * blk), 1)
    lane = lax.broadcasted_iota(jnp.int32, (blk, LANES), 1)
    bias_ref[0] = jnp.where(kj <= qi, 0.0, NEG_BIG)
    bias_ref[1] = jnp.where((kj >= qi) & (kj <= qi + blk), 0.0, NEG_BIG)
    bias_ref[2] = jnp.where((kj >= blk) & (kj <= qi + blk), 0.0, NEG_BIG)

    def block(i, c):
        r0 = pl.multiple_of(i * blk, blk)
        k0 = pl.multiple_of(jnp.maximum(r0 - blk, 0), blk)
        bias = bias_ref[jnp.where(i == 0, 0, jnp.where(i % nblk == 0, 2, 1))]
        lse_tile = jnp.zeros((blk, LANES), F32)
        for h in range(ATTN_HB):
            s = lax.dot_general(rows(qs, h, r0, blk), rows(ks, h, k0, 2 * blk), NT_DIMS, preferred_element_type=F32)
            s = s + bias
            m = jnp.max(s, axis=-1, keepdims=True)
            p = jnp.exp2(s - m).astype(BF16)
            vx = jnp.concatenate([rows(vs, h, k0, 2 * blk), jnp.ones((2 * blk, LANES), BF16)], axis=1)
            r = jnp.dot(p, vx, preferred_element_type=F32)
            l = r[:, LANES:]
            o = r[:, :LANES] * (1.0 / l)
            if direct:
                o_ref[0, pl.ds(r0, blk), hsl(h)] = o.astype(o_ref.dtype)
            else:
                orm_ref[h, pl.ds(r0, blk), :] = o
            lse_tile = jnp.where(lane == h, m + jnp.log(l) * (1.0 / LN2), lse_tile)
        if direct:
            lse_ref[0, pl.ds(r0, blk), :] = lse_tile
        else:
            lrm_ref[pl.ds(r0, blk), :] = lse_tile
        return c

    lax.fori_loop(0, seq // blk, block, 0, unroll=True)

    if not direct:
        for h in range(ATTN_HB):
            merge_rows(orm_ref.at[h])
            o_ref[0, :, hsl(h)] = nat_ref[...].astype(o_ref.dtype)
        merge_rows(lrm_ref)
        lse_ref[0] = nat_ref[...]


def _attention_group(src3, units, dilation):
    batch, seq, _ = src3.shape
    halves = ATTN_HEADS // ATTN_HB
    wb = ATTN_HB * ATTN_HEAD_DIM
    col = lambda unit: (lambda b, hh: (b, 0, unit * halves + hh))
    scratch = [] if dilation == 1 else [
        pltpu.VMEM((seq, LANES), F32),
        pltpu.VMEM((seq, LANES), F32),
        pltpu.VMEM((seq, wb), BF16),
        pltpu.VMEM((seq, wb), BF16),
        pltpu.VMEM((seq, wb), BF16),
        pltpu.VMEM((ATTN_HB, seq, LANES), F32),
        pltpu.VMEM((seq, LANES), F32),
    ]
    out, lse = pl.pallas_call(
        functools.partial(_attn_kernel, dilation=dilation),
        grid=(batch, halves),
        in_specs=[pl.BlockSpec((1, seq, wb), col(u)) for u in units],
        out_specs=[
            pl.BlockSpec((1, seq, wb), lambda b, hh: (b, 0, hh)),
            pl.BlockSpec((1, seq, LANES), lambda b, hh: (b, 0, hh)),
        ],
        out_shape=[
            jax.ShapeDtypeStruct((batch, seq, ATTN_OUT), BF16),
            jax.ShapeDtypeStruct((batch, seq, halves * LANES), F32),
        ],
        scratch_shapes=[pltpu.VMEM((3, ATTN_BLOCK, 2 * ATTN_BLOCK), F32)] + scratch,
        compiler_params=_cparams("parallel", "parallel"),
        name=f"attn_d{dilation}",
    )(src3, src3, src3)
    return out.reshape(batch * seq, ATTN_OUT), lse.reshape(batch * seq, halves * LANES)


def _rope_tables(seq):
    half = ROPE_DIM // 2
    inv = ROPE_THETA ** (-jnp.arange(0, ROPE_DIM, 2, dtype=F32) / ROPE_DIM)
    ang = jnp.arange(seq, dtype=F32)[:, None] * inv[None, :]
    c, s = jnp.cos(ang), jnp.sin(ang)
    gap = LANES // 2 - half
    one, zero = jnp.ones((seq, gap), F32), jnp.zeros((seq, gap), F32)
    cos = jnp.concatenate([c, one, c, one], axis=1)
    sin = jnp.concatenate([-s, zero, s, zero], axis=1)
    return cos, sin


def _merge_kernel(yssd_ref, o0_ref, o1_ref, o2_ref, l0_ref, l1_ref, l2_ref, gs_ref, ga_ref, h_ref,
                  wa_ref, wb_ref, wo_ref, nw_ref, hout_ref, u_ref, yattn_ref):
    l0, l1, l2 = l0_ref[...], l1_ref[...], l2_ref[...]
    m = jnp.maximum(jnp.maximum(l0, l1), l2)
    e0, e1, e2 = jnp.exp2(l0 - m), jnp.exp2(l1 - m), jnp.exp2(l2 - m)
    inv = 1.0 / (e0 + e1 + e2)
    w0, w1, w2 = e0 * inv, e1 * inv, e2 * inv
    for h in range(ATTN_HEADS):
        sl = slice(h * LANES, (h + 1) * LANES)
        c = (h // ATTN_HB) * LANES + h % ATTN_HB
        yh = (w0[:, c:c + 1] * o0_ref[:, sl].astype(F32) + w1[:, c:c + 1] * o1_ref[:, sl].astype(F32)
              + w2[:, c:c + 1] * o2_ref[:, sl].astype(F32))
        yattn_ref[:, sl] = yh.astype(BF16)
    a = jnp.dot(yssd_ref[...], wa_ref[...], preferred_element_type=F32)
    b = jnp.dot(yattn_ref[...], wb_ref[...], preferred_element_type=F32)
    merged = _sigmoid(gs_ref[...].astype(F32)) * a + _sigmoid(ga_ref[...].astype(F32)) * b
    hn = h_ref[...] + jnp.dot(merged.astype(BF16), wo_ref[...], preferred_element_type=F32)
    hout_ref[...] = hn
    u_ref[...] = (_rms_scale(hn) * nw_ref[...]).astype(u_ref.dtype)


def _resident(a):
    return pl.BlockSpec(a.shape, lambda *_: (0,) * a.ndim, pipeline_mode=pl.Buffered(1))


def _merge(y_ssd, outs, lses, pack, h, w_a, w_b, w_o, norm_w, tm=512):
    t = h.shape[0]
    d = D_MODEL
    rows = lambda w, col=0: pl.BlockSpec((tm, w), lambda i, col=col: (i, col))
    nw = norm_w.reshape(1, d)
    lw = lses[0].shape[1]
    return pl.pallas_call(
        _merge_kernel,
        grid=(t // tm,),
        in_specs=[rows(SSD_INNER), rows(d), rows(d), rows(d), rows(lw), rows(lw), rows(lw),
                  rows(d, PACK_GS), rows(d, PACK_GA), rows(d), _resident(w_a), _resident(w_b), _resident(w_o),
                  _resident(nw)],
        out_specs=[rows(d), rows(d)],
        out_shape=[jax.ShapeDtypeStruct((t, d), F32), jax.ShapeDtypeStruct((t, d), BF16)],
        scratch_shapes=[pltpu.VMEM((tm, ATTN_OUT), BF16)],
        compiler_params=_cparams("parallel"),
        name="merge",
    )(y_ssd, *outs, *lses, pack, pack, h, w_a, w_b, w_o, nw)


FFN_TH = 256


def _ffn_kernel(u_ref, wgu_ref, wd_ref, h_ref, nw_ref, *rest, last):
    if last:
        out_ref, acc_ref = rest
    else:
        hout_ref, u_out_ref, acc_ref = rest
    acc_ref[...] = h_ref[...]
    u = u_ref[...]

    def hidden_step(j, c):
        lo = pl.multiple_of(j * FFN_TH, FFN_TH)
        gate = jnp.dot(u, wgu_ref[:, pl.ds(lo, FFN_TH)], preferred_element_type=F32)
        up = jnp.dot(u, wgu_ref[:, pl.ds(lo + FFN_HIDDEN, FFN_TH)], preferred_element_type=F32)
        act = (_silu(gate) * up).astype(BF16)
        acc_ref[...] += jnp.dot(act, wd_ref[pl.ds(lo, FFN_TH), :], preferred_element_type=F32)
        return c

    lax.fori_loop(0, FFN_HIDDEN // FFN_TH, hidden_step, 0, unroll=True)
    hn = acc_ref[...]
    normed = _rms_scale(hn) * nw_ref[...]
    if last:
        out_ref[...] = normed
    else:
        hout_ref[...] = hn
        u_out_ref[...] = normed.astype(u_out_ref.dtype)


def _ffn(u, w_gu, w_d, h, norm_w, last, tm=1024):
    t, d = u.shape
    rows = lambda: pl.BlockSpec((tm, d), lambda i: (i, 0))
    nw = norm_w.reshape(1, d)
    if last:
        out_specs = rows()
        out_shape = jax.ShapeDtypeStruct((t, d), F32)
    else:
        out_specs = [rows(), rows()]
        out_shape = [jax.ShapeDtypeStruct((t, d), F32), jax.ShapeDtypeStruct((t, d), BF16)]
    return pl.pallas_call(
        functools.partial(_ffn_kernel, last=last),
        grid=(t // tm,),
        in_specs=[rows(), _resident(w_gu), _resident(w_d), rows(), _resident(nw)],
        out_specs=out_specs,
        out_shape=out_shape,
        scratch_shapes=[pltpu.VMEM((tm, d), F32)],
        compiler_params=_cparams("parallel"),
        name="ffn_last" if last else "ffn",
    )(u, w_gu, w_d, h, nw)


def kernel(x, norm_mix, w_in, conv_w, conv_b, dt_bias, a_log, d_skip, ssd_norm, w_ssd_branch, w_attn_branch,
           w_out, norm_ffn, w_gate_up, w_down, norm_final):
    batch, seq, d = x.shape
    depth = w_in.shape[0]
    t = batch * seq
    tables = _rope_tables(seq)
    h = x.reshape(t, d)
    u = _rmsnorm(h, norm_mix[0])
    for layer in range(depth):
        w_pack, w_dt = _pack_in_weights(w_in, layer)
        pack, dt_raw = _inproj(u, w_pack, w_dt, tables)
        y_ssd = _ssd(pack, dt_raw, conv_w[layer], conv_b[layer], dt_bias[layer], a_log[layer], d_skip[layer],
                     ssd_norm[layer], batch, seq)
        assert [w // d_ for w, d_ in ATTN_PATTERNS] == [ATTN_BLOCK] * 3 and [d_ for _, d_ in ATTN_PATTERNS] == [1, 4, 16]
        pack3 = pack.reshape(batch, seq, PACK_W)
        outs, lses = zip(*[_attention_group(pack3, (PACK_Q + g, PACK_K, PACK_V), dilation)
                           for g, (_, dilation) in enumerate(ATTN_PATTERNS)])
        h, u_ffn = _merge(y_ssd, outs, lses, pack, h, w_ssd_branch[layer].astype(BF16),
                          w_attn_branch[layer].astype(BF16), w_out[layer].astype(BF16), norm_ffn[layer])
        last = layer == depth - 1
        nxt = norm_final if last else norm_mix[layer + 1]
        res = _ffn(u_ffn, w_gate_up[layer].astype(BF16), w_down[layer].astype(BF16), h, nxt, last)
        if last:
            return res.reshape(batch, seq, d)
        h, u = res
```

```python
import functools
import math

import jax
import jax.numpy as jnp
from jax import lax
from jax.experimental import pallas as pl
from jax.experimental.pallas import tpu as pltpu

F32 = jnp.float32
BF16 = jnp.bfloat16

D_MODEL = 1024
RMS_EPS = 1e-5

SSD_INNER = 2048
SSD_HEAD_DIM = 64
SSD_HEADS = 32
SSD_STATE = 128
SSD_GROUPS = 4
SSD_CONV = 4
SSD_CHUNK = 128
SSD_BC = 2 * SSD_GROUPS * SSD_STATE
SSD_CONV_CH = SSD_INNER + SSD_BC

ATTN_HEAD_DIM = 128
ATTN_HEADS = 8
ATTN_PATTERNS = ((128, 1), (512, 4), (2048, 16))
ATTN_BLOCK = 128
ROPE_THETA = 500000.0
ROPE_DIM = 32
ATTN_OUT = ATTN_HEADS * ATTN_HEAD_DIM

FFN_HIDDEN = 2816

PACK_W = 12 * 1024
PACK_Q, PACK_Z, PACK_K, PACK_V, PACK_GS, PACK_GA = 3, 6, 8, 9, 10, 11

Q_SCALE = ATTN_HEAD_DIM ** -0.5 * math.log2(math.e)
LN2 = math.log(2.0)

LANES = 128
NEG_BIG = -1e30
VMEM_LIMIT = 56 * 1024 * 1024


def _cparams(*sem):
    return pltpu.CompilerParams(dimension_semantics=sem, vmem_limit_bytes=VMEM_LIMIT)


def _sigmoid(x):
    return 1.0 / (1.0 + jnp.exp(-x))


def _silu(x):
    h = 0.5 * x
    return h + h * jnp.tanh(h)


def _softplus(x):
    return jnp.maximum(x, 0.0) + jnp.log(1.0 + jnp.exp(-jnp.abs(x)))


def _rms_scale(x):
    return x * lax.rsqrt(jnp.mean(x * x, axis=-1, keepdims=True) + RMS_EPS)


def _rmsnorm_kernel(x_ref, w_ref, o_ref):
    o_ref[...] = (_rms_scale(x_ref[...]) * w_ref[...]).astype(o_ref.dtype)


def _rmsnorm(x, w, tm=1024):
    t, d = x.shape
    return pl.pallas_call(
        _rmsnorm_kernel,
        grid=(t // tm,),
        in_specs=[pl.BlockSpec((tm, d), lambda i: (i, 0)), pl.BlockSpec((1, d), lambda i: (0, 0))],
        out_specs=pl.BlockSpec((tm, d), lambda i: (i, 0)),
        out_shape=jax.ShapeDtypeStruct((t, d), BF16),
        compiler_params=_cparams("parallel"),
        name="rmsnorm",
    )(x, w.reshape(1, d))


IN_DT_COL = SSD_INNER + SSD_CONV_CH
NT_DIMS = (((1,), (1,)), ((), ()))


def _pack_kernel(a_ref, o_ref):
    j = pl.program_id(0)
    rotary = (j >= PACK_Q) & (j < PACK_Z) | (j == PACK_K)
    half, mid = ROPE_DIM // 2, ATTN_HEAD_DIM // 2

    @pl.when(jnp.logical_not(rotary))
    def _():
        o_ref[...] = a_ref[0].astype(o_ref.dtype)

    @pl.when(rotary)
    def _():
        for h in range(a_ref.shape[1] // ATTN_HEAD_DIM):
            b = h * ATTN_HEAD_DIM
            for dst, src, n in ((0, 0, half), (half, mid, half), (2 * half, 2 * half, mid - 2 * half),
                                (mid, half, half), (mid + half, mid + half, mid - half)):
                o_ref[b + dst:b + dst + n, :] = a_ref[0, b + src:b + src + n, :].astype(o_ref.dtype)


def _pack_in_weights(w_in, layer):
    _, d, _ = w_in.shape
    unit = ATTN_OUT
    w_t = jnp.swapaxes(w_in, 1, 2)
    late = IN_DT_COL + SSD_HEADS
    g = SSD_HEADS
    ug, lg = unit // g, late // g
    src = lambda j: g * jnp.where(j < PACK_Q, SSD_INNER // g + j * ug,
                                  jnp.where(j < PACK_Z, lg + (j - PACK_Q) * ug,
                                            jnp.where(j < PACK_K, (j - PACK_Z) * ug, lg + (j - PACK_Q - 2) * ug)))
    packed = pl.pallas_call(
        _pack_kernel,
        grid=(PACK_W // unit,),
        in_specs=[pl.BlockSpec((pl.Element(1), pl.Element(unit), pl.Element(d)), lambda j: (layer, src(j), 0))],
        out_specs=pl.BlockSpec((unit, d), lambda j: (j, 0)),
        out_shape=jax.ShapeDtypeStruct((PACK_W, d), BF16),
        compiler_params=_cparams("arbitrary"),
        name="pack_w_in",
    )(w_t)
    w_dt = jnp.pad(w_t[layer, IN_DT_COL:late, :], ((0, LANES - SSD_HEADS), (0, 0)))
    return packed, w_dt


def _inproj_kernel(u_ref, w_ref, wdt_ref, cos_ref, sin_ref, o_ref, dt_ref):
    j = pl.program_id(1)
    is_q = (j >= PACK_Q) & (j < PACK_Q + len(ATTN_PATTERNS))
    rotary = is_q | (j == PACK_K)

    @pl.when(rotary)
    def _():
        res = lax.dot_general(u_ref[...], w_ref[...], NT_DIMS, preferred_element_type=F32)
        scale = jnp.where(is_q, Q_SCALE, 1.0)
        cos, sin = cos_ref[...] * scale, sin_ref[...] * scale
        for h in range(res.shape[1] // LANES):
            x = res[:, h * LANES:(h + 1) * LANES]
            o_ref[:, h * LANES:(h + 1) * LANES] = (x * cos + pltpu.roll(x, LANES // 2, 1) * sin).astype(o_ref.dtype)

    @pl.when(jnp.logical_not(rotary))
    def _():
        o_ref[...] = lax.dot_general(u_ref[...], w_ref[...], NT_DIMS, preferred_element_type=F32).astype(o_ref.dtype)

    @pl.when(j == 0)
    def _():
        dt_ref[...] = lax.dot_general(u_ref[...], wdt_ref[...].astype(BF16), NT_DIMS, preferred_element_type=F32)


def _inproj(u, w_pack, w_dt, tables, tm=2048):
    t, d = u.shape
    n = w_pack.shape[0]
    tn = ATTN_OUT
    seq = tables[0].shape[0]
    tab_spec = pl.BlockSpec((tm, LANES), lambda i, j: (i % (seq // tm), 0))
    return pl.pallas_call(
        _inproj_kernel,
        grid=(t // tm, n // tn),
        in_specs=[
            pl.BlockSpec((tm, d), lambda i, j: (i, 0)),
            pl.BlockSpec((tn, d), lambda i, j: (j, 0)),
            pl.BlockSpec((LANES, d), lambda i, j: (0, 0)),
            tab_spec, tab_spec,
        ],
        out_specs=[
            pl.BlockSpec((tm, tn), lambda i, j: (i, j)),
            pl.BlockSpec((tm, LANES), lambda i, j: (i, 0)),
        ],
        out_shape=[jax.ShapeDtypeStruct((t, n), BF16), jax.ShapeDtypeStruct((t, LANES), F32)],
        compiler_params=_cparams("parallel", "arbitrary"),
        name="inproj",
    )(u, w_pack, w_dt, *tables)


SSD_NB = 1


def _ssd_kernel(xbc_ref, z_ref, dt_ref, *refs):
    convw_ref, convb_ref, dtb_ref, alog_ref, dskip_ref, normw_ref = refs[:6]
    o_ref = refs[6]
    xbuf_ref, state_ref, conv_ref, y_ref, acum_ref, rowm_ref = refs[7:]
    for s in range(SSD_NB):
        _ssd_conv(xbc_ref.at[0, s], convw_ref, convb_ref, xbuf_ref.at[s], state_ref.at[s], conv_ref.at[s])
    for s in range(SSD_NB):
        _ssd_scan(z_ref.at[0, s], dt_ref.at[0, s], dtb_ref, alog_ref, dskip_ref, normw_ref, o_ref.at[0, s],
                  state_ref.at[s], conv_ref.at[s], y_ref.at[s], acum_ref.at[s], rowm_ref.at[s])


def _ssd_conv(xbc_ref, convw_ref, convb_ref, xbuf_ref, state_ref, conv_ref):
    q = SSD_CHUNK

    @pl.when(pl.program_id(1) == 0)
    def _():
        xbuf_ref[0:q, :] = jnp.zeros((q, SSD_CONV_CH), BF16)
        state_ref[...] = jnp.zeros_like(state_ref)

    xbuf_ref[q:2 * q, :] = xbc_ref[...]
    ti = lax.broadcasted_iota(jnp.int32, (3 * q, 2 * q), 0)
    si = lax.broadcasted_iota(jnp.int32, (3 * q, 2 * q), 1)
    tap = lax.shift_right_logical(ti, q.bit_length() - 1)
    shift = jnp.where(si == (ti & (q - 1)) + (q - 3) + tap, 1.0, 0.0).astype(BF16)
    cw = 512

    def conv_body(i, c):
        cs = pl.ds(pl.multiple_of(i * cw, cw), cw)
        xx = xbuf_ref[:, cs]
        sh = jnp.dot(shift, xx, preferred_element_type=F32)
        w = convw_ref[:, cs]
        acc = (convb_ref[:, cs] + w[3:4] * xx[q:2 * q].astype(F32) + w[0:1] * sh[0:q] + w[1:2] * sh[q:2 * q]
               + w[2:3] * sh[2 * q:3 * q])
        conv_ref[:, cs] = _silu(acc)
        return c

    lax.fori_loop(0, SSD_CONV_CH // cw, conv_body, 0, unroll=True)
    xbuf_ref[0:q, :] = xbuf_ref[q:2 * q, :]


def _ssd_scan(z_ref, dt_ref, dtb_ref, alog_ref, dskip_ref, normw_ref, o_ref, state_ref, conv_ref, y_ref,
              acum_ref, rowm_ref):
    q = SSD_CHUNK
    hpg = SSD_HEADS // SSD_GROUPS

    dt = _softplus(dt_ref[...] + dtb_ref[...])
    adt = dt * (-jnp.exp(alog_ref[...]))
    rowi = lax.broadcasted_iota(jnp.int32, (q, q), 0)
    coli = lax.broadcasted_iota(jnp.int32, (q, q), 1)
    acum = adt
    for s in (1, 2, 4, 8, 16, 32, 64):
        acum = acum + jnp.where(rowi >= s, pltpu.roll(acum, s, 0), 0.0)
    acum = acum * (1.0 / LN2)
    rowm_ref[...] = acum.T - jnp.log(dt.T) * (1.0 / LN2)
    for g in range(SSD_GROUPS):
        acum_ref[g] = pltpu.roll(acum, (LANES - hpg * g) % LANES, 1)
    causal = rowi >= coli
    left = coli < SSD_HEAD_DIM

    def group_body(g, c):
        a_g = acum_ref[g]
        rm_g = rowm_ref[pl.ds(pl.multiple_of(g * hpg, hpg), hpg), :]
        b_g = conv_ref[:, pl.ds(pl.multiple_of(SSD_INNER + g * SSD_STATE, LANES), SSD_STATE)]
        c_g = conv_ref[:, pl.ds(pl.multiple_of(SSD_INNER + (SSD_GROUPS + g) * SSD_STATE, LANES), SSD_STATE)]
        c_gb = c_g.astype(BF16)
        cb = lax.dot_general(c_gb, b_g.astype(BF16), (((1,), (1,)), ((), ())), preferred_element_type=F32)
        b_t = b_g.T
        for pp in range(hpg // 2):
            ps = pl.ds(pl.multiple_of((g * (hpg // 2) + pp) * LANES, LANES), LANES)
            m_parts, s_parts, ecols, ealasts = [], [], [], []
            for j in (2 * pp, 2 * pp + 1):
                col = a_g[:, j:j + 1]
                rowm = rm_g[j:j + 1, :]
                alast = a_g[q - 1:q, j:j + 1]
                m_parts.append(cb * jnp.exp2(jnp.where(causal, col - rowm, NEG_BIG)))
                s_parts.append(b_t * jnp.exp2(alast - rowm))
                ecols.append(jnp.exp2(col))
                ealasts.append(jnp.exp2(alast))
            lhs = jnp.concatenate([jnp.concatenate(m_parts, axis=1), jnp.concatenate(s_parts, axis=1)],
                                  axis=0).astype(BF16)
            xp = conv_ref[:, ps]
            rhs = jnp.concatenate([jnp.where(left, xp, 0.0), jnp.where(left, 0.0, xp)], axis=0).astype(BF16)
            res = jnp.dot(lhs, rhs, preferred_element_type=F32)
            st_old = state_ref[:, ps]
            y_off = jnp.dot(c_gb, st_old.astype(BF16), preferred_element_type=F32)
            y_ref[:, ps] = res[0:q] + y_off * jnp.where(left, ecols[0], ecols[1]) + dskip_ref[:, ps] * xp
            state_ref[:, ps] = st_old * jnp.where(left, ealasts[0], ealasts[1]) + res[q:2 * q]
        return c

    lax.fori_loop(0, SSD_GROUPS, group_body, 0, unroll=4)

    gw = SSD_INNER // SSD_GROUPS

    def norm_body(g, c):
        gs = pl.ds(pl.multiple_of(g * gw, gw), gw)
        yg = y_ref[:, gs] * _silu(z_ref[:, gs].astype(F32))
        o_ref[:, gs] = (_rms_scale(yg) * normw_ref[:, gs]).astype(o_ref.dtype)
        return c

    lax.fori_loop(0, SSD_GROUPS, norm_body, 0, unroll=4)


def _ssd(pack, dt_raw, conv_w, conv_b, dt_bias, a_log, d_skip, norm_w, batch, seq):
    t = pack.shape[0]
    nc = seq // SSD_CHUNK
    q = SSD_CHUNK
    pad = LANES - SSD_HEADS
    nb = SSD_NB
    const = lambda shape: pl.BlockSpec(shape, lambda b, c: (0, 0))
    rows = lambda width, col: pl.BlockSpec((1, nb, q, width), lambda b, c: (b, 0, c, col))
    pack4 = pack.reshape(batch // nb, nb, seq, PACK_W)
    out = pl.pallas_call(
        _ssd_kernel,
        grid=(batch // nb, nc),
        in_specs=[
            rows(SSD_CONV_CH, 0), rows(SSD_INNER, PACK_Z // 2), rows(LANES, 0),
            const((SSD_CONV, SSD_CONV_CH)), const((1, SSD_CONV_CH)), const((1, LANES)), const((1, LANES)),
            const((1, SSD_INNER)), const((1, SSD_INNER)),
        ],
        out_specs=rows(SSD_INNER, 0),
        out_shape=jax.ShapeDtypeStruct((batch // nb, nb, seq, SSD_INNER), BF16),
        scratch_shapes=[
            pltpu.VMEM((nb, 2 * q, SSD_CONV_CH), BF16),
            pltpu.VMEM((nb, SSD_STATE, SSD_INNER), F32),
            pltpu.VMEM((nb, q, SSD_CONV_CH), F32),
            pltpu.VMEM((nb, q, SSD_INNER), F32),
            pltpu.VMEM((nb, SSD_GROUPS, q, LANES), F32),
            pltpu.VMEM((nb, LANES, q), F32),
        ],
        compiler_params=_cparams("parallel", "arbitrary"),
        name="ssd",
    )(pack4, pack4, dt_raw.reshape(batch // nb, nb, seq, LANES), conv_w, conv_b.reshape(1, -1),
      jnp.pad(dt_bias, (0, pad)).reshape(1, LANES), jnp.pad(a_log, (0, pad)).reshape(1, LANES),
      jnp.repeat(d_skip, SSD_HEAD_DIM).reshape(1, SSD_INNER), norm_w.reshape(1, SSD_INNER))
    return out.reshape(t, SSD_INNER)


ATTN_HB = 4


def _attn_kernel(q_ref, k_ref, v_ref, o_ref, lse_ref, bias_ref, *scratch, dilation):
    assert dilation in (1, 4, 16)
    blk = ATTN_BLOCK
    seq = q_ref.shape[1]
    length = seq // dilation
    nblk = length // blk
    quarter = seq // 4
    direct = dilation == 1
    if not direct:
        nat_ref, tmp_ref, qrm_ref, krm_ref, vrm_ref, orm_ref, lrm_ref = scratch
    hsl = lambda h: slice(h * LANES, (h + 1) * LANES)

    def split_rows(src_ref, dst_ref, h):
        nat_ref[...] = src_ref[0, :, hsl(h)].astype(F32)
        if dilation == 4:
            for r in range(4):
                dst_ref[r * quarter:(r + 1) * quarter, hsl(h)] = (
                    nat_ref[pl.ds(r, quarter, stride=4), :].astype(BF16))
        else:
            for r in range(4):
                tmp_ref[r * quarter:(r + 1) * quarter, :] = nat_ref[pl.ds(r, quarter, stride=4), :]
            for r in range(4):
                for r2 in range(4):
                    c = r * 4 + r2
                    dst_ref[c * length:(c + 1) * length, hsl(h)] = (
                        tmp_ref[pl.ds(r * quarter + r2, length, stride=4), :].astype(BF16))

    def merge_rows(src):
        if dilation == 4:
            for r in range(4):
                nat_ref[pl.ds(r, quarter, stride=4), :] = src[r * quarter:(r + 1) * quarter, :]
        else:
            for r in range(4):
                for r2 in range(4):
                    c = r * 4 + r2
                    tmp_ref[pl.ds(r * quarter + r2, length, stride=4), :] = src[c * length:(c + 1) * length, :]
            for r in range(4):
                nat_ref[pl.ds(r, quarter, stride=4), :] = tmp_ref[r * quarter:(r + 1) * quarter, :]

    if direct:
        rows = lambda ref, h, r0, n: ref[0, pl.ds(r0, n), hsl(h)]
        qs, ks, vs = q_ref, k_ref, v_ref
    else:
        for h in range(ATTN_HB):
            split_rows(q_ref, qrm_ref, h)
            split_rows(k_ref, krm_ref, h)
            split_rows(v_ref, vrm_ref, h)
        rows = lambda ref, h, r0, n: ref[pl.ds(r0, n), hsl(h)]
        qs, ks, vs = qrm_ref, krm_ref, vrm_ref

    qi = lax.broadcasted_iota(jnp.int32, (blk, 2 * blk), 0)
    kj = lax.broadcasted_iota(jnp.int32, (blk, 2 * blk), 1)
    lane = lax.broadcasted_iota(jnp.int32, (blk, LANES), 1)
    bias_ref[0] = jnp.where(kj <= qi, 0.0, NEG_BIG)
    bias_ref[1] = jnp.where((kj >= qi) & (kj <= qi + blk), 0.0, NEG_BIG)
    bias_ref[2] = jnp.where((kj >= blk) & (kj <= qi + blk), 0.0, NEG_BIG)

    def block(i, c):
        r0 = pl.multiple_of(i * blk, blk)
        k0 = pl.multiple_of(jnp.maximum(r0 - blk, 0), blk)
        bias = bias_ref[jnp.where(i == 0, 0, jnp.where(i % nblk == 0, 2, 1))]
        lse_tile = jnp.zeros((blk, LANES), F32)
        for h in range(ATTN_HB):
            s = lax.dot_general(rows(qs, h, r0, blk), rows(ks, h, k0, 2 * blk), NT_DIMS, preferred_element_type=F32)
            s = s + bias
            m = jnp.max(s, axis=-1, keepdims=True)
            p = jnp.exp2(s - m).astype(BF16)
            vx = jnp.concatenate([rows(vs, h, k0, 2 * blk), jnp.ones((2 * blk, LANES), BF16)], axis=1)
            r = jnp.dot(p, vx, preferred_element_type=F32)
            l = r[:, LANES:]
            o = r[:, :LANES] * (1.0 / l)
            if direct:
                o_ref[0, pl.ds(r0, blk), hsl(h)] = o.astype(o_ref.dtype)
            else:
                orm_ref[h, pl.ds(r0, blk), :] = o
            lse_tile = jnp.where(lane == h, m + jnp.log(l) * (1.0 / LN2), lse_tile)
        if direct:
            lse_ref[0, pl.ds(r0, blk), :] = lse_tile
        else:
            lrm_ref[pl.ds(r0, blk), :] = lse_tile
        return c

    lax.fori_loop(0, seq // blk, block, 0, unroll=True)

    if not direct:
        for h in range(ATTN_HB):
            merge_rows(orm_ref.at[h])
            o_ref[0, :, hsl(h)] = nat_ref[...].astype(o_ref.dtype)
        merge_rows(lrm_ref)
        lse_ref[0] = nat_ref[...]


def _attention_group(src3, units, dilation):
    batch, seq, _ = src3.shape
    halves = ATTN_HEADS // ATTN_HB
    wb = ATTN_HB * ATTN_HEAD_DIM
    col = lambda unit: (lambda b, hh: (b, 0, unit * halves + hh))
    scratch = [] if dilation == 1 else [
        pltpu.VMEM((seq, LANES), F32),
        pltpu.VMEM((seq, LANES), F32),
        pltpu.VMEM((seq, wb), BF16),
        pltpu.VMEM((seq, wb), BF16),
        pltpu.VMEM((seq, wb), BF16),
        pltpu.VMEM((ATTN_HB, seq, LANES), F32),
        pltpu.VMEM((seq, LANES), F32),
    ]
    out, lse = pl.pallas_call(
        functools.partial(_attn_kernel, dilation=dilation),
        grid=(batch, halves),
        in_specs=[pl.BlockSpec((1, seq, wb), col(u)) for u in units],
        out_specs=[
            pl.BlockSpec((1, seq, wb), lambda b, hh: (b, 0, hh)),
            pl.BlockSpec((1, seq, LANES), lambda b, hh: (b, 0, hh)),
        ],
        out_shape=[
            jax.ShapeDtypeStruct((batch, seq, ATTN_OUT), BF16),
            jax.ShapeDtypeStruct((batch, seq, halves * LANES), F32),
        ],
        scratch_shapes=[pltpu.VMEM((3, ATTN_BLOCK, 2 * ATTN_BLOCK), F32)] + scratch,
        compiler_params=_cparams("parallel", "parallel"),
        name=f"attn_d{dilation}",
    )(src3, src3, src3)
    return out.reshape(batch * seq, ATTN_OUT), lse.reshape(batch * seq, halves * LANES)


def _rope_tables(seq):
    half = ROPE_DIM // 2
    inv = ROPE_THETA ** (-jnp.arange(0, ROPE_DIM, 2, dtype=F32) / ROPE_DIM)
    ang = jnp.arange(seq, dtype=F32)[:, None] * inv[None, :]
    c, s = jnp.cos(ang), jnp.sin(ang)
    gap = LANES // 2 - half
    one, zero = jnp.ones((seq, gap), F32), jnp.zeros((seq, gap), F32)
    cos = jnp.concatenate([c, one, c, one], axis=1)
    sin = jnp.concatenate([-s, zero, s, zero], axis=1)
    return cos, sin


def _merge_kernel(yssd_ref, o0_ref, o1_ref, o2_ref, l0_ref, l1_ref, l2_ref, gs_ref, ga_ref, h_ref,
                  wa_ref, wb_ref, wo_ref, nw_ref, hout_ref, u_ref, yattn_ref):
    l0, l1, l2 = l0_ref[...], l1_ref[...], l2_ref[...]
    m = jnp.maximum(jnp.maximum(l0, l1), l2)
    e0, e1, e2 = jnp.exp2(l0 - m), jnp.exp2(l1 - m), jnp.exp2(l2 - m)
    inv = 1.0 / (e0 + e1 + e2)
    w0, w1, w2 = e0 * inv, e1 * inv, e2 * inv
    for h in range(ATTN_HEADS):
        sl = slice(h * LANES, (h + 1) * LANES)
        c = (h // ATTN_HB) * LANES + h % ATTN_HB
        yh = (w0[:, c:c + 1] * o0_ref[:, sl].astype(F32) + w1[:, c:c + 1] * o1_ref[:, sl].astype(F32)
              + w2[:, c:c + 1] * o2_ref[:, sl].astype(F32))
        yattn_ref[:, sl] = yh.astype(BF16)
    a = jnp.dot(yssd_ref[...], wa_ref[...], preferred_element_type=F32)
    b = jnp.dot(yattn_ref[...], wb_ref[...], preferred_element_type=F32)
    merged = _sigmoid(gs_ref[...].astype(F32)) * a + _sigmoid(ga_ref[...].astype(F32)) * b
    hn = h_ref[...] + jnp.dot(merged.astype(BF16), wo_ref[...], preferred_element_type=F32)
    hout_ref[...] = hn
    u_ref[...] = (_rms_scale(hn) * nw_ref[...]).astype(u_ref.dtype)


def _resident(a):
    return pl.BlockSpec(a.shape, lambda *_: (0,) * a.ndim, pipeline_mode=pl.Buffered(1))


def _merge(y_ssd, outs, lses, pack, h, w_a, w_b, w_o, norm_w, tm=512):
    t = h.shape[0]
    d = D_MODEL
    rows = lambda w, col=0: pl.BlockSpec((tm, w), lambda i, col=col: (i, col))
    nw = norm_w.reshape(1, d)
    lw = lses[0].shape[1]
    return pl.pallas_call(
        _merge_kernel,
        grid=(t // tm,),
        in_specs=[rows(SSD_INNER), rows(d), rows(d), rows(d), rows(lw), rows(lw), rows(lw),
                  rows(d, PACK_GS), rows(d, PACK_GA), rows(d), _resident(w_a), _resident(w_b), _resident(w_o),
                  _resident(nw)],
        out_specs=[rows(d), rows(d)],
        out_shape=[jax.ShapeDtypeStruct((t, d), F32), jax.ShapeDtypeStruct((t, d), BF16)],
        scratch_shapes=[pltpu.VMEM((tm, ATTN_OUT), BF16)],
        compiler_params=_cparams("parallel"),
        name="merge",
    )(y_ssd, *outs, *lses, pack, pack, h, w_a, w_b, w_o, nw)


FFN_TH = 256


def _ffn_kernel(u_ref, wgu_ref, wd_ref, h_ref, nw_ref, *rest, last):
    if last:
        out_ref, acc_ref = rest
    else:
        hout_ref, u_out_ref, acc_ref = rest
    acc_ref[...] = h_ref[...]
    u = u_ref[...]

    def hidden_step(j, c):
        lo = pl.multiple_of(j * FFN_TH, FFN_TH)
        gate = jnp.dot(u, wgu_ref[:, pl.ds(lo, FFN_TH)], preferred_element_type=F32)
        up = jnp.dot(u, wgu_ref[:, pl.ds(lo + FFN_HIDDEN, FFN_TH)], preferred_element_type=F32)
        act = (_silu(gate) * up).astype(BF16)
        acc_ref[...] += jnp.dot(act, wd_ref[pl.ds(lo, FFN_TH), :], preferred_element_type=F32)
        return c

    lax.fori_loop(0, FFN_HIDDEN // FFN_TH, hidden_step, 0, unroll=True)
    hn = acc_ref[...]
    normed = _rms_scale(hn) * nw_ref[...]
    if last:
        out_ref[...] = normed
    else:
        hout_ref[...] = hn
        u_out_ref[...] = normed.astype(u_out_ref.dtype)


def _ffn(u, w_gu, w_d, h, norm_w, last, tm=1024):
    t, d = u.shape
    rows = lambda: pl.BlockSpec((tm, d), lambda i: (i, 0))
    nw = norm_w.reshape(1, d)
    if last:
        out_specs = rows()
        out_shape = jax.ShapeDtypeStruct((t, d), F32)
    else:
        out_specs = [rows(), rows()]
        out_shape = [jax.ShapeDtypeStruct((t, d), F32), jax.ShapeDtypeStruct((t, d), BF16)]
    return pl.pallas_call(
        functools.partial(_ffn_kernel, last=last),
        grid=(t // tm,),
        in_specs=[rows(), _resident(w_gu), _resident(w_d), rows(), _resident(nw)],
        out_specs=out_specs,
        out_shape=out_shape,
        scratch_shapes=[pltpu.VMEM((tm, d), F32)],
        compiler_params=_cparams("parallel"),
        name="ffn_last" if last else "ffn",
    )(u, w_gu, w_d, h, nw)


def kernel(x, norm_mix, w_in, conv_w, conv_b, dt_bias, a_log, d_skip, ssd_norm, w_ssd_branch, w_attn_branch,
           w_out, norm_ffn, w_gate_up, w_down, norm_final):
    batch, seq, d = x.shape
    depth = w_in.shape[0]
    t = batch * seq
    tables = _rope_tables(seq)
    h = x.reshape(t, d)
    u = _rmsnorm(h, norm_mix[0])
    for layer in range(depth):
        w_pack, w_dt = _pack_in_weights(w_in, layer)
        pack, dt_raw = _inproj(u, w_pack, w_dt, tables)
        y_ssd = _ssd(pack, dt_raw, conv_w[layer], conv_b[layer], dt_bias[layer], a_log[layer], d_skip[layer],
                     ssd_norm[layer], batch, seq)
        assert [w // d_ for w, d_ in ATTN_PATTERNS] == [ATTN_BLOCK] * 3 and [d_ for _, d_ in ATTN_PATTERNS] == [1, 4, 16]
        pack3 = pack.reshape(batch, seq, PACK_W)
        outs, lses = zip(*[_attention_group(pack3, (PACK_Q + g, PACK_K, PACK_V), dilation)
                           for g, (_, dilation) in enumerate(ATTN_PATTERNS)])
        h, u_ffn = _merge(y_ssd, outs, lses, pack, h, w_ssd_branch[layer].astype(BF16),
                          w_attn_branch[layer].astype(BF16), w_out[layer].astype(BF16), norm_ffn[layer])
        last = layer == depth - 1
        nxt = norm_final if last else norm_mix[layer + 1]
        res = _ffn(u_ffn, w_gate_up[layer].astype(BF16), w_down[layer].astype(BF16), h, nxt, last)
        if last:
            return res.reshape(batch, seq, d)
        h, u = res
```
